```python
import math
import jax, jax.numpy as jnp
from jax import lax
import numpy as np

D_MODEL = 2048
BATCH = 4
SEQ = 2048
DEPTH = 4
DEC_BATCH = 128
DEC_SEQ = 4
PAST_LEN = 16384
PAGE_SIZE = 128

N_MIXERS = 4
D_MIX = D_MODEL
D_GROUP = D_MIX // N_MIXERS
N_SUB = 4
D_SUB = D_GROUP // N_SUB
N_PROJ = 8
CONV_A_W = 31
POOL_WINDOWS = (2, 4, 8, 16)
POOL_BUF = max(POOL_WINDOWS) - 1
SCONV_W = 3
CHUNK = 128
D_FF = int(math.ceil(D_MODEL * 8 / 3 / 256) * 256)
EPS = 1e-6

kernel_name = "hybrid_conv_pool_sconv_sgu_decoder_step"


def rmsnorm(x, g):
    xf = x.astype(jnp.float32)
    y = xf * lax.rsqrt(jnp.mean(xf * xf, axis=-1, keepdims=True) + EPS)
    return (y * g.astype(jnp.float32)).astype(x.dtype)


def layernorm(x, g, b):
    xf = x.astype(jnp.float32)
    mu = jnp.mean(xf, axis=-1, keepdims=True)
    var = jnp.mean(jnp.square(xf - mu), axis=-1, keepdims=True)
    y = (xf - mu) * lax.rsqrt(var + EPS)
    return (y * g.astype(jnp.float32) + b.astype(jnp.float32)).astype(x.dtype)


def causal_dwconv(buf, z, w):
    width, c = w.shape
    zp = jnp.concatenate([buf.astype(z.dtype), z], axis=1)
    y = lax.conv_general_dilated(zp, w[:, None, :].astype(z.dtype), window_strides=(1,), padding='VALID',
                                 dimension_numbers=('NWC', 'WIO', 'NWC'), feature_group_count=c)
    return y, zp[:, -(width - 1):]


def causal_multiscale_pool(buf, p, pos):
    b, l, c = p.shape
    pp = jnp.concatenate([buf.astype(p.dtype), p], axis=1)
    cs = jnp.cumsum(pp.astype(jnp.float32), axis=1)
    cs = jnp.concatenate([jnp.zeros((b, 1, c), jnp.float32), cs], axis=1)
    off = POOL_BUF + 1
    means = []
    for gi, w in enumerate(POOL_WINDOWS):
        sl = slice(gi * D_SUB, (gi + 1) * D_SUB)
        s = cs[:, off:off + l, sl] - cs[:, off - w:off - w + l, sl]
        cnt = jnp.minimum(pos + 1, w).astype(jnp.float32)[None, :, None]
        means.append(s / cnt)
    mean = jnp.concatenate(means, axis=-1)
    return (mean - p.astype(jnp.float32)).astype(p.dtype), pp[:, -POOL_BUF:]


def chunk_spatial_mix(v, w_s, b_s):
    b, l, c = v.shape
    pad = (-l) % CHUNK
    vp = jnp.pad(v, ((0, 0), (0, pad), (0, 0)))
    n = (l + pad) // CHUNK
    vr = vp.reshape(b, n, CHUNK, N_SUB, D_SUB)
    mask = jnp.tril(jnp.ones((CHUNK, CHUNK), v.dtype))
    z = jnp.einsum('gij,bnjgc->bnigc', w_s * mask[None], vr)
    z = z + jnp.transpose(b_s)[None, None, :, :, None]
    return z.reshape(b, n * CHUNK, c)[:, :l]


def mixing_sublayer(x, buf_a, buf_p, buf_c, pos, norm_mix_g, w_in, conv_a_w, conv_a_b, ln_a_g, ln_a_b,
                    pool_w, pool_scale, sconv_w, sgu_ln_g, sgu_ln_b, sgu_w, sgu_b, out_norm_g, w_out):
    b, l, _ = x.shape
    xn = rmsnorm(x, norm_mix_g)
    proj = jnp.einsum('bld,de->ble', xn, w_in)
    a_val, a_gate, p, c_h, c_b, c_c, d_u, d_v = jnp.split(proj, N_PROJ, axis=-1)
    glu = a_val * jax.nn.sigmoid(a_gate)
    ya, nbuf_a = causal_dwconv(buf_a, glu, conv_a_w)
    ya = jax.nn.silu(layernorm(ya + conv_a_b, ln_a_g, ln_a_b))
    pooled, nbuf_p = causal_multiscale_pool(buf_p, p, pos)
    yb = jnp.einsum('blgc,gcd->blgd', pooled.reshape(b, l, N_SUB, D_SUB), pool_w).reshape(b, l, D_GROUP)
    yb = yb * pool_scale
    yc_conv, nbuf_c = causal_dwconv(buf_c, c_c * c_h, sconv_w)
    yc = c_b * yc_conv
    vn = layernorm(d_v, sgu_ln_g, sgu_ln_b)
    yd = d_u * chunk_spatial_mix(vn, sgu_w, sgu_b)
    mix = jnp.stack([ya, yb, yc, yd], axis=2)
    mix = rmsnorm(mix, out_norm_g.reshape(N_MIXERS, D_GROUP)).reshape(b, l, D_MIX)
    x = x + jnp.einsum('ble,ed->bld', mix, w_out)
    return x, nbuf_a, nbuf_p, nbuf_c, vn


def ffn_sublayer(x, norm_ffn_g, w_gate, w_up, w_down):
    xn = rmsnorm(x, norm_ffn_g)
    h = jax.nn.silu(jnp.einsum('bld,df->blf', xn, w_gate)) * jnp.einsum('bld,df->blf', xn, w_up)
    return x + jnp.einsum('blf,fd->bld', h, w_down)


def setup_inputs(seed: int = 0) -> dict:
    key = jax.random.key(seed)
    ks = jax.random.split(key, 32)
    f32 = jnp.float32
    nrm = lambda k, shape, s: jax.random.normal(k, shape, f32) * s
    return {
        "x_prompt": nrm(ks[0], (BATCH, SEQ, D_MODEL), 1.0),
        "x_sample": nrm(ks[1], (DEC_BATCH, DEC_SEQ, D_MODEL), 1.0),
        "state_conv_a": nrm(ks[2], (DEPTH, DEC_BATCH, CONV_A_W - 1, D_GROUP), 0.5),
        "state_pool": nrm(ks[3], (DEPTH, DEC_BATCH, POOL_BUF, D_GROUP), 1.0),
        "state_sconv": nrm(ks[4], (DEPTH, DEC_BATCH, SCONV_W - 1, D_GROUP), 0.5),
        "norm_mix_g": 1.0 + nrm(ks[5], (DEPTH, D_MODEL), 0.05),
        "w_in": nrm(ks[6], (DEPTH, D_MODEL, N_PROJ * D_GROUP), D_MODEL ** -0.5),
        "conv_a_w": nrm(ks[7], (DEPTH, CONV_A_W, D_GROUP), CONV_A_W ** -0.5),
        "conv_a_b": nrm(ks[8], (DEPTH, D_GROUP), 0.02),
        "ln_a_g": 1.0 + nrm(ks[9], (DEPTH, D_GROUP), 0.05),
        "ln_a_b": nrm(ks[10], (DEPTH, D_GROUP), 0.02),
        "pool_w": nrm(ks[11], (DEPTH, N_SUB, D_SUB, D_SUB), D_SUB ** -0.5),
        "pool_scale": 1.0 + nrm(ks[12], (DEPTH, D_GROUP), 0.1),
        "sconv_w": nrm(ks[13], (DEPTH, SCONV_W, D_GROUP), SCONV_W ** -0.5),
        "sgu_ln_g": 1.0 + nrm(ks[14], (DEPTH, D_GROUP), 0.05),
        "sgu_ln_b": nrm(ks[15], (DEPTH, D_GROUP), 0.02),
        "sgu_w": nrm(ks[16], (DEPTH, N_SUB, CHUNK, CHUNK), CHUNK ** -0.5),
        "sgu_b": 1.0 + nrm(ks[17], (DEPTH, N_SUB, CHUNK), 0.05),
        "out_norm_g": 1.0 + nrm(ks[18], (DEPTH, D_MIX), 0.05),
        "w_out": nrm(ks[19], (DEPTH, D_MIX, D_MODEL), D_MIX ** -0.5),
        "norm_ffn_g": 1.0 + nrm(ks[20], (DEPTH, D_MODEL), 0.05),
        "w_gate": nrm(ks[21], (DEPTH, D_MODEL, D_FF), D_MODEL ** -0.5),
        "w_up": nrm(ks[22], (DEPTH, D_MODEL, D_FF), D_MODEL ** -0.5),
        "w_down": nrm(ks[23], (DEPTH, D_FF, D_MODEL), D_FF ** -0.5),
        "final_norm_g": 1.0 + nrm(ks[24], (D_MODEL,), 0.05),
    }


def reference(x_prompt, x_sample, state_conv_a, state_pool, state_sconv, norm_mix_g, w_in, conv_a_w,
              conv_a_b, ln_a_g, ln_a_b, pool_w, pool_scale, sconv_w, sgu_ln_g, sgu_ln_b, sgu_w, sgu_b,
              out_norm_g, w_out, norm_ffn_g, w_gate, w_up, w_down, final_norm_g):
    b_p, l_p, _ = x_prompt.shape
    pos_p = jnp.arange(l_p, dtype=jnp.int32)
    pos_s = PAST_LEN + jnp.arange(x_sample.shape[1], dtype=jnp.int32)
    dt = x_prompt.dtype
    zero_a = jnp.zeros((b_p, CONV_A_W - 1, D_GROUP), dt)
    zero_p = jnp.zeros((b_p, POOL_BUF, D_GROUP), dt)
    zero_c = jnp.zeros((b_p, SCONV_W - 1, D_GROUP), dt)
    hp, hs = x_prompt, x_sample
    ca_p, ca_s, pl_p, pl_s, sc_p, sc_s, v_s = [], [], [], [], [], [], []
    for l in range(DEPTH):
        params = (norm_mix_g[l], w_in[l], conv_a_w[l], conv_a_b[l], ln_a_g[l], ln_a_b[l], pool_w[l],
                  pool_scale[l], sconv_w[l], sgu_ln_g[l], sgu_ln_b[l], sgu_w[l], sgu_b[l], out_norm_g[l],
                  w_out[l])
        hp, na, npl, nc, _ = mixing_sublayer(hp, zero_a, zero_p, zero_c, pos_p, *params)
        ca_p.append(na); pl_p.append(npl); sc_p.append(nc)
        hs, na, npl, nc, vn = mixing_sublayer(hs, state_conv_a[l], state_pool[l], state_sconv[l], pos_s, *params)
        ca_s.append(na); pl_s.append(npl); sc_s.append(nc); v_s.append(vn)
        hp = ffn_sublayer(hp, norm_ffn_g[l], w_gate[l], w_up[l], w_down[l])
        hs = ffn_sublayer(hs, norm_ffn_g[l], w_gate[l], w_up[l], w_down[l])
    y_prompt = rmsnorm(hp, final_norm_g)
    y_sample = rmsnorm(hs, final_norm_g)
    return (y_prompt, y_sample, jnp.stack(ca_p), jnp.stack(ca_s), jnp.stack(pl_p), jnp.stack(pl_s),
            jnp.stack(sc_p), jnp.stack(sc_s), jnp.stack(v_s))
```

```python
import functools

import jax
import jax.numpy as jnp
from jax import lax
from jax.experimental import pallas as pl
from jax.experimental.pallas import tpu as pltpu

D_MODEL = 2048
BATCH = 4
SEQ = 2048
DEPTH = 4
DEC_BATCH = 128
DEC_SEQ = 4
PAST_LEN = 16384
D_GROUP = 512
N_SUB = 4
D_SUB = 128
N_PROJ = 8
D_PROJ = N_PROJ * D_GROUP
CONV_A_W = 31
POOL_WINDOWS = (2, 4, 8, 16)
POOL_BUF = 15
SCONV_W = 3
CHUNK = 128
D_FF = 5632
EPS = 1e-6

M_P = BATCH * SEQ
M_S = DEC_BATCH * DEC_SEQ
M_ALL = M_P + M_S

F32 = jnp.float32
BF16 = jnp.bfloat16

V7X_VMEM_LIMIT_BYTES = 56 * 1024 * 1024

BM = 1088
NORM_ROWS = 64
BN_IN = 1024
BN_OUT = 1024
BN_FF = 512
BN_DOWN = 512

TS = 256
RC = 64
HALO_A = 32
HALO_P = 16
HALO_C = 8

BB = 32


def _params(sem, vmem=V7X_VMEM_LIMIT_BYTES):
    return pltpu.CompilerParams(dimension_semantics=sem, vmem_limit_bytes=vmem)


def _rmsnorm_rows_to_bf16(x_ref, g_ref, xn_ref):
    g = g_ref[...]

    def body(i, carry):
        r = pl.multiple_of(i * NORM_ROWS, NORM_ROWS)
        x = x_ref[pl.ds(r, NORM_ROWS), :]
        ms = jnp.mean(x * x, axis=-1, keepdims=True)
        xn_ref[pl.ds(r, NORM_ROWS), :] = (x * lax.rsqrt(ms + EPS) * g).astype(BF16)
        return carry

    lax.fori_loop(0, BM // NORM_ROWS, body, 0)


def _in_proj_kernel(x_ref, g_ref, w_ref, o_ref, xn_ref):
    @pl.when(pl.program_id(1) == 0)
    def _():
        _rmsnorm_rows_to_bf16(x_ref, g_ref, xn_ref)

    o_ref[...] = jnp.dot(xn_ref[...], w_ref[...], preferred_element_type=F32)


def _in_proj(x, g, w, layer):
    return pl.pallas_call(
        _in_proj_kernel,
        grid=(M_ALL // BM, D_PROJ // BN_IN),
        in_specs=[
            pl.BlockSpec((BM, D_MODEL), lambda i, j: (i, 0)),
            pl.BlockSpec((None, 1, D_MODEL), lambda i, j: (layer, 0, 0)),
            pl.BlockSpec((None, D_MODEL, BN_IN), lambda i, j: (layer, 0, j)),
        ],
        out_specs=pl.BlockSpec((BM, BN_IN), lambda i, j: (i, j)),
        out_shape=jax.ShapeDtypeStruct((M_ALL, D_PROJ), F32),
        scratch_shapes=[pltpu.VMEM((BM, D_MODEL), BF16)],
        compiler_params=_params(("arbitrary", "arbitrary")),
        name=f"in_proj_l{layer}",
    )(x, g, w)


def _residual_matmul_kernel(a_ref, w_ref, r_ref, o_ref):
    o_ref[...] = r_ref[...] + jnp.dot(a_ref[...], w_ref[...], preferred_element_type=F32)


def _residual_matmul(a, w, res, layer, bn, name):
    k = a.shape[1]
    return pl.pallas_call(
        _residual_matmul_kernel,
        grid=(M_ALL // BM, D_MODEL // bn),
        in_specs=[
            pl.BlockSpec((BM, k), lambda i, j: (i, 0)),
            pl.BlockSpec((None, k, bn), lambda i, j: (layer, 0, j)),
            pl.BlockSpec((BM, bn), lambda i, j: (i, j)),
        ],
        out_specs=pl.BlockSpec((BM, bn), lambda i, j: (i, j)),
        out_shape=jax.ShapeDtypeStruct((M_ALL, D_MODEL), F32),
        compiler_params=_params(("arbitrary", "arbitrary")),
        name=f"{name}_l{layer}",
    )(a, w, res)


def _ffn_up_kernel(x_ref, g_ref, wg_ref, wu_ref, o_ref, xn_ref):
    @pl.when(pl.program_id(1) == 0)
    def _():
        _rmsnorm_rows_to_bf16(x_ref, g_ref, xn_ref)

    xn = xn_ref[...]
    gate = jnp.dot(xn, wg_ref[...], preferred_element_type=F32)
    up = jnp.dot(xn, wu_ref[...], preferred_element_type=F32)
    o_ref[...] = (jax.nn.silu(gate) * up).astype(BF16)


def _ffn_up(x, g, wg, wu, layer):
    return pl.pallas_call(
        _ffn_up_kernel,
        grid=(M_ALL // BM, D_FF // BN_FF),
        in_specs=[
            pl.BlockSpec((BM, D_MODEL), lambda i, j: (i, 0)),
            pl.BlockSpec((None, 1, D_MODEL), lambda i, j: (layer, 0, 0)),
            pl.BlockSpec((None, D_MODEL, BN_FF), lambda i, j: (layer, 0, j)),
            pl.BlockSpec((None, D_MODEL, BN_FF), lambda i, j: (layer, 0, j)),
        ],
        out_specs=pl.BlockSpec((BM, BN_FF), lambda i, j: (i, j)),
        out_shape=jax.ShapeDtypeStruct((M_ALL, D_FF), BF16),
        scratch_shapes=[pltpu.VMEM((BM, D_MODEL), BF16)],
        compiler_params=_params(("arbitrary", "arbitrary")),
        name=f"ffn_up_l{layer}",
    )(x, g, wg, wu)


def _layernorm(x, g, b):
    mu = jnp.mean(x, axis=-1, keepdims=True)
    xc = x - mu
    var = jnp.mean(xc * xc, axis=-1, keepdims=True)
    return xc * lax.rsqrt(var + EPS) * g + b


def _rms(x, g):
    return x * lax.rsqrt(jnp.mean(x * x, axis=-1, keepdims=True) + EPS) * g


def _group(g):
    return slice(g * D_SUB, (g + 1) * D_SUB)


def _mixer(m):
    return slice(m * D_GROUP, (m + 1) * D_GROUP)


def _mixer_prompt_kernel(proj_ref, caw_ref, cab_ref, lag_ref, lab_ref, pw_ref, ps_ref, scw_ref,
                         sg_ref, sb_ref, sw_ref, sbias_ref, ong_ref,
                         mix_ref, sta_ref, stp_ref, stc_ref,
                         gbuf, pbuf, cbuf, vbuf):
    s = pl.program_id(1)

    @pl.when(s == 0)
    def _():
        gbuf[0:HALO_A, :] = jnp.zeros((HALO_A, D_GROUP), F32)
        pbuf[0:HALO_P, :] = jnp.zeros((HALO_P, D_GROUP), F32)
        cbuf[0:HALO_C, :] = jnp.zeros((HALO_C, D_GROUP), F32)

    @pl.when(s > 0)
    def _():
        gbuf[0:HALO_A, :] = gbuf[TS:TS + HALO_A, :]
        pbuf[0:HALO_P, :] = pbuf[TS:TS + HALO_P, :]
        cbuf[0:HALO_C, :] = cbuf[TS:TS + HALO_C, :]

    sg = sg_ref[...]
    sb = sb_ref[...]
    for c in range(TS // RC):
        r0 = c * RC

        def col(k, r0=r0):
            return proj_ref[r0:r0 + RC, k * D_GROUP:(k + 1) * D_GROUP]

        gbuf[HALO_A + r0:HALO_A + r0 + RC, :] = col(0) * jax.nn.sigmoid(col(1))
        pbuf[HALO_P + r0:HALO_P + r0 + RC, :] = col(2)
        cbuf[HALO_C + r0:HALO_C + r0 + RC, :] = col(5) * col(3)
        vbuf[r0:r0 + RC, :] = _layernorm(col(7), sg, sb)

    cab = cab_ref[...]
    lag = lag_ref[...]
    lab = lab_ref[...]
    ps = ps_ref[...]
    for c in range(TS // RC):
        r0 = c * RC

        acc = None
        for r in range(8):
            part = None
            for q in range(4):
                wi = CONV_A_W - 1 - 8 * q - r
                if wi < 0:
                    continue
                lo = r0 + HALO_A - 8 - 8 * q
                term = caw_ref[wi:wi + 1, :] * gbuf[lo:lo + RC + 8, :]
                part = term if part is None else part + term
            if r:
                part = pltpu.roll(part, r, axis=0)
            part = part[8:8 + RC, :]
            acc = part if acc is None else acc + part
        ya = _layernorm(acc + cab, lag, lab)
        ya = ya * jax.nn.sigmoid(ya)
        mix_ref[r0:r0 + RC, _mixer(0)] = _rms(ya, ong_ref[:, _mixer(0)]).astype(BF16)

        pos = s * TS + r0 + lax.broadcasted_iota(jnp.int32, (RC, D_SUB), 0)
        yb = []
        for g, w in enumerate(POOL_WINDOWS):
            hist = pbuf[r0:r0 + RC + HALO_P, _group(g)]
            tot = hist
            span = 1
            while span < w:
                tot = tot + pltpu.roll(tot, span, axis=0)
                span *= 2
            cnt = jnp.minimum(pos + 1, w).astype(F32)
            pooled = tot[HALO_P:, :] / cnt - hist[HALO_P:, :]
            yb.append(jnp.dot(pooled.astype(BF16), pw_ref[g].astype(BF16), preferred_element_type=F32))
        yb = jnp.concatenate(yb, axis=-1) * ps
        mix_ref[r0:r0 + RC, _mixer(1)] = _rms(yb, ong_ref[:, _mixer(1)]).astype(BF16)

        hist = cbuf[r0:r0 + RC + HALO_C, :]
        conv = (scw_ref[2:3, :] * hist + scw_ref[1:2, :] * pltpu.roll(hist, 1, axis=0)
                + scw_ref[0:1, :] * pltpu.roll(hist, 2, axis=0))
        yc = proj_ref[r0:r0 + RC, 4 * D_GROUP:5 * D_GROUP] * conv[HALO_C:, :]
        mix_ref[r0:r0 + RC, _mixer(2)] = _rms(yc, ong_ref[:, _mixer(2)]).astype(BF16)

    row = lax.broadcasted_iota(jnp.int32, (CHUNK, CHUNK), 0)
    colid = lax.broadcasted_iota(jnp.int32, (CHUNK, CHUNK), 1)
    tril = (row >= colid).astype(F32)
    wm = [(sw_ref[g] * tril).astype(BF16) for g in range(N_SUB)]
    for c in range(TS // CHUNK):
        r0 = c * CHUNK
        v = vbuf[r0:r0 + CHUNK, :].astype(BF16)
        z = [jnp.dot(wm[g], v[:, _group(g)], preferred_element_type=F32) + sbias_ref[:, g:g + 1]
             for g in range(N_SUB)]
        yd = proj_ref[r0:r0 + CHUNK, 6 * D_GROUP:7 * D_GROUP] * jnp.concatenate(z, axis=-1)
        mix_ref[r0:r0 + CHUNK, _mixer(3)] = _rms(yd, ong_ref[:, _mixer(3)]).astype(BF16)

    @pl.when(s == pl.num_programs(1) - 1)
    def _():
        sta_ref[...] = pltpu.roll(gbuf[TS:TS + HALO_A, :], CONV_A_W - 1, axis=0)[0:CONV_A_W - 1, :]
        stp_ref[...] = pltpu.roll(pbuf[TS:TS + HALO_P, :], POOL_BUF, axis=0)[0:POOL_BUF, :]
        stc_ref[...] = pltpu.roll(cbuf[TS:TS + HALO_C, :], SCONV_W - 1, axis=0)[0:SCONV_W - 1, :]


def _mixer_prompt(proj, p, layer):
    n_s = SEQ // TS

    def lp(shape):
        nd = len(shape)
        return pl.BlockSpec((None,) + shape, lambda b, s: (layer,) + (0,) * nd)

    state = lambda n: pl.BlockSpec((None, n, D_GROUP), lambda b, s: (b, 0, 0))
    return pl.pallas_call(
        _mixer_prompt_kernel,
        grid=(BATCH, n_s),
        in_specs=[
            pl.BlockSpec((TS, D_PROJ), lambda b, s: (b * n_s + s, 0)),
            lp((CONV_A_W, D_GROUP)), lp((1, D_GROUP)), lp((1, D_GROUP)), lp((1, D_GROUP)),
            lp((N_SUB, D_SUB, D_SUB)), lp((1, D_GROUP)), lp((SCONV_W, D_GROUP)),
            lp((1, D_GROUP)), lp((1, D_GROUP)), lp((N_SUB, CHUNK, CHUNK)), lp((CHUNK, N_SUB)),
            lp((1, D_MODEL)),
        ],
        out_specs=[
            pl.BlockSpec((TS, D_MODEL), lambda b, s: (b * n_s + s, 0)),
            state(CONV_A_W - 1), state(POOL_BUF), state(SCONV_W - 1),
        ],
        out_shape=[
            jax.ShapeDtypeStruct((M_ALL, D_MODEL), BF16),
            jax.ShapeDtypeStruct((BATCH, CONV_A_W - 1, D_GROUP), F32),
            jax.ShapeDtypeStruct((BATCH, POOL_BUF, D_GROUP), F32),
            jax.ShapeDtypeStruct((BATCH, SCONV_W - 1, D_GROUP), F32),
        ],
        scratch_shapes=[
            pltpu.VMEM((HALO_A + TS, D_GROUP), F32),
            pltpu.VMEM((HALO_P + TS, D_GROUP), F32),
            pltpu.VMEM((HALO_C + TS, D_GROUP), F32),
            pltpu.VMEM((TS, D_GROUP), F32),
        ],
        compiler_params=_params(("arbitrary", "arbitrary")),
        name=f"mixer_prompt_l{layer}",
    )(proj, p["conv_a_w"], p["conv_a_b"], p["ln_a_g"], p["ln_a_b"], p["pool_w"], p["pool_scale"],
      p["sconv_w"], p["sgu_ln_g"], p["sgu_ln_b"], p["sgu_w"], p["sgu_b_t"], p["out_norm_g"])


def _mixer_sample_kernel(wsm_ref, bsm_ref, proj_ref, sa_ref, sp_ref, sc_ref,
                         caw_ref, cab_ref, lag_ref, lab_ref, pw_ref, ps_ref, scw_ref,
                         sg_ref, sb_ref, ong_ref, mix_in_ref,
                         mix_ref, nsa_ref, nsp_ref, nsc_ref, vn_ref, *, layer):
    del mix_in_ref

    def col(t, k):
        return proj_ref[t, :, k * D_GROUP:(k + 1) * D_GROUP]

    def slab(j):
        return slice(j * D_GROUP, (j + 1) * D_GROUP)

    steps = range(DEC_SEQ)

    glu = [col(t, 0) * jax.nn.sigmoid(col(t, 1)) for t in steps]
    n_a = CONV_A_W - 1

    def za(j):
        return sa_ref[:, slab(j)] if j < n_a else glu[j - n_a]

    cab = cab_ref[...]
    lag = lag_ref[...]
    lab = lab_ref[...]
    for t in steps:
        acc = caw_ref[0:1, :] * za(t)
        for k in range(1, CONV_A_W):
            acc = acc + caw_ref[k:k + 1, :] * za(t + k)
        ya = _layernorm(acc + cab, lag, lab)
        ya = ya * jax.nn.sigmoid(ya)
        mix_ref[t, :, _mixer(0)] = _rms(ya, ong_ref[:, _mixer(0)]).astype(BF16)
    for j in range(n_a):
        nsa_ref[:, slab(j)] = za(j + DEC_SEQ)

    pin = [col(t, 2) for t in steps]

    def zp(j):
        return sp_ref[:, slab(j)] if j < POOL_BUF else pin[j - POOL_BUF]

    ps = ps_ref[...]
    for t in steps:
        yb = []
        for g, w in enumerate(POOL_WINDOWS):
            cnt = float(min(PAST_LEN + t + 1, w))
            tot = zp(POOL_BUF + t)[:, _group(g)]
            for d in range(1, w):
                tot = tot + zp(POOL_BUF + t - d)[:, _group(g)]
            pooled = tot / cnt - pin[t][:, _group(g)]
            yb.append(jnp.dot(pooled.astype(BF16), pw_ref[g].astype(BF16), preferred_element_type=F32))
        yb = jnp.concatenate(yb, axis=-1) * ps
        mix_ref[t, :, _mixer(1)] = _rms(yb, ong_ref[:, _mixer(1)]).astype(BF16)
    for j in range(POOL_BUF):
        nsp_ref[:, slab(j)] = zp(j + DEC_SEQ)

    gated = [col(t, 5) * col(t, 3) for t in steps]
    n_c = SCONV_W - 1

    def zc(j):
        return sc_ref[:, slab(j)] if j < n_c else gated[j - n_c]

    for t in steps:
        conv = scw_ref[0:1, :] * zc(t) + scw_ref[1:2, :] * zc(t + 1) + scw_ref[2:3, :] * zc(t + 2)
        yc = col(t, 4) * conv
        mix_ref[t, :, _mixer(2)] = _rms(yc, ong_ref[:, _mixer(2)]).astype(BF16)
    for j in range(n_c):
        nsc_ref[:, slab(j)] = zc(j + DEC_SEQ)

    sg = sg_ref[...]
    sb = sb_ref[...]
    vn = [_layernorm(col(t, 7), sg, sb) for t in steps]
    for t in steps:
        vn_ref[:, slab(t)] = vn[t]
    for t in steps:
        z = []
        for g in range(N_SUB):
            base = (layer * N_SUB + g) * DEC_SEQ * DEC_SEQ + t * DEC_SEQ
            zg = wsm_ref[base] * vn[0][:, _group(g)]
            for j in range(1, t + 1):
                zg = zg + wsm_ref[base + j] * vn[j][:, _group(g)]
            z.append(zg + bsm_ref[(layer * N_SUB + g) * DEC_SEQ + t])
        yd = col(t, 6) * jnp.concatenate(z, axis=-1)
        mix_ref[t, :, _mixer(3)] = _rms(yd, ong_ref[:, _mixer(3)]).astype(BF16)


def _mixer_sample(proj, mix, states, p, layer):
    sa, sp, sc = states
    slabs = M_ALL // DEC_BATCH
    first = M_P // DEC_BATCH // DEC_SEQ
    proj3 = proj.reshape(slabs, DEC_BATCH, D_PROJ)
    mix3 = mix.reshape(slabs, DEC_BATCH, D_MODEL)

    def lp(shape):
        nd = len(shape)
        return pl.BlockSpec((None,) + shape, lambda i: (layer,) + (0,) * nd)

    smem = pl.BlockSpec(memory_space=pltpu.SMEM)
    st_in = lambda n: pl.BlockSpec((None, BB, n * D_GROUP), lambda i: (layer, i, 0))
    st_out = lambda n: pl.BlockSpec((BB, n * D_GROUP), lambda i: (i, 0))
    n_a, n_c = CONV_A_W - 1, SCONV_W - 1
    outs = pl.pallas_call(
        functools.partial(_mixer_sample_kernel, layer=layer),
        grid=(DEC_BATCH // BB,),
        in_specs=[
            smem, smem,
            pl.BlockSpec((DEC_SEQ, BB, D_PROJ), lambda i: (first, i, 0)),
            st_in(n_a), st_in(POOL_BUF), st_in(n_c),
            lp((CONV_A_W, D_GROUP)), lp((1, D_GROUP)), lp((1, D_GROUP)), lp((1, D_GROUP)),
            lp((N_SUB, D_SUB, D_SUB)), lp((1, D_GROUP)), lp((SCONV_W, D_GROUP)),
            lp((1, D_GROUP)), lp((1, D_GROUP)), lp((1, D_MODEL)),
            pl.BlockSpec(memory_space=pl.ANY),
        ],
        out_specs=[
            pl.BlockSpec((DEC_SEQ, BB, D_MODEL), lambda i: (first, i, 0)),
            st_out(n_a), st_out(POOL_BUF), st_out(n_c), st_out(DEC_SEQ),
        ],
        out_shape=[
            jax.ShapeDtypeStruct((slabs, DEC_BATCH, D_MODEL), BF16),
            jax.ShapeDtypeStruct((DEC_BATCH, n_a * D_GROUP), F32),
            jax.ShapeDtypeStruct((DEC_BATCH, POOL_BUF * D_GROUP), F32),
            jax.ShapeDtypeStruct((DEC_BATCH, n_c * D_GROUP), F32),
            jax.ShapeDtypeStruct((DEC_BATCH, DEC_SEQ * D_GROUP), F32),
        ],
        input_output_aliases={16: 0},
        compiler_params=_params(("arbitrary",)),
        name=f"mixer_sample_l{layer}",
    )(p["sgu_w_small"], p["sgu_b_small"], proj3, sa, sp, sc,
      p["conv_a_w"], p["conv_a_b"], p["ln_a_g"], p["ln_a_b"], p["pool_w"], p["pool_scale"],
      p["sconv_w"], p["sgu_ln_g"], p["sgu_ln_b"], p["out_norm_g"], mix3)
    mix3, nsa, nsp, nsc, vn = outs
    return (mix3.reshape(M_ALL, D_MODEL),
            nsa.reshape(DEC_BATCH, n_a, D_GROUP), nsp.reshape(DEC_BATCH, POOL_BUF, D_GROUP),
            nsc.reshape(DEC_BATCH, n_c, D_GROUP), vn.reshape(DEC_BATCH, DEC_SEQ, D_GROUP))


FINAL_ROWS = 512


def _final_norm_prompt_kernel(x_ref, g_ref, o_ref):
    o_ref[...] = _rms(x_ref[...], g_ref[...])


def _final_norm_sample_kernel(x_ref, g_ref, o_ref):
    g = g_ref[...]
    for t in range(DEC_SEQ):
        o_ref[:, t * D_MODEL:(t + 1) * D_MODEL] = _rms(x_ref[t], g)


def _final_norm(x, g):
    y_p = pl.pallas_call(
        _final_norm_prompt_kernel,
        grid=(M_P // FINAL_ROWS,),
        in_specs=[pl.BlockSpec((FINAL_ROWS, D_MODEL), lambda i: (i, 0)),
                  pl.BlockSpec((1, D_MODEL), lambda i: (0, 0))],
        out_specs=pl.BlockSpec((FINAL_ROWS, D_MODEL), lambda i: (i, 0)),
        out_shape=jax.ShapeDtypeStruct((M_P, D_MODEL), F32),
        compiler_params=_params(("arbitrary",)),
        name="final_norm_prompt",
    )(x, g)
    x3 = x.reshape(M_ALL // DEC_BATCH, DEC_BATCH, D_MODEL)
    y_s = pl.pallas_call(
        _final_norm_sample_kernel,
        grid=(1,),
        in_specs=[pl.BlockSpec((DEC_SEQ, DEC_BATCH, D_MODEL), lambda i: (M_P // DEC_BATCH // DEC_SEQ, 0, 0)),
                  pl.BlockSpec((1, D_MODEL), lambda i: (0, 0))],
        out_specs=pl.BlockSpec((DEC_BATCH, DEC_SEQ * D_MODEL), lambda i: (0, 0)),
        out_shape=jax.ShapeDtypeStruct((DEC_BATCH, DEC_SEQ * D_MODEL), F32),
        compiler_params=_params(("arbitrary",)),
        name="final_norm_sample",
    )(x3, g)
    return y_p.reshape(BATCH, SEQ, D_MODEL), y_s.reshape(DEC_BATCH, DEC_SEQ, D_MODEL)


def kernel(x_prompt, x_sample, state_conv_a, state_pool, state_sconv, norm_mix_g, w_in, conv_a_w, conv_a_b, ln_a_g, ln_a_b, pool_w, pool_scale, sconv_w, sgu_ln_g, sgu_ln_b, sgu_w, sgu_b, out_norm_g, w_out, norm_ffn_g, w_gate, w_up, w_down, final_norm_g):
    row = lambda a: a.reshape(DEPTH, 1, a.shape[-1])
    p = {
        "conv_a_w": conv_a_w, "conv_a_b": row(conv_a_b), "ln_a_g": row(ln_a_g), "ln_a_b": row(ln_a_b),
        "pool_w": pool_w, "pool_scale": row(pool_scale), "sconv_w": sconv_w,
        "sgu_ln_g": row(sgu_ln_g), "sgu_ln_b": row(sgu_ln_b), "sgu_w": sgu_w,
        "sgu_b_t": jnp.swapaxes(sgu_b, 1, 2),
        "sgu_w_small": sgu_w[:, :, :DEC_SEQ, :DEC_SEQ].reshape(-1),
        "sgu_b_small": sgu_b[:, :, :DEC_SEQ].reshape(-1),
        "out_norm_g": row(out_norm_g),
    }
    g_mix = row(norm_mix_g)
    g_ffn = row(norm_ffn_g)
    w_in_b, w_out_b = w_in.astype(BF16), w_out.astype(BF16)
    w_gate_b, w_up_b, w_down_b = w_gate.astype(BF16), w_up.astype(BF16), w_down.astype(BF16)

    states = (state_conv_a.reshape(DEPTH, DEC_BATCH, -1), state_pool.reshape(DEPTH, DEC_BATCH, -1),
              state_sconv.reshape(DEPTH, DEC_BATCH, -1))

    x = jnp.concatenate([x_prompt.reshape(M_P, D_MODEL),
                         jnp.swapaxes(x_sample, 0, 1).reshape(M_S, D_MODEL)], axis=0)

    ca_p, ca_s, pl_p, pl_s, sc_p, sc_s, v_s = [], [], [], [], [], [], []
    for l in range(DEPTH):
        proj = _in_proj(x, g_mix, w_in_b, l)
        mix, a_p, p_p, c_p = _mixer_prompt(proj, p, l)
        mix, a_s, p_s, c_s, vn = _mixer_sample(proj, mix, states, p, l)
        ca_p.append(a_p); pl_p.append(p_p); sc_p.append(c_p)
        ca_s.append(a_s); pl_s.append(p_s); sc_s.append(c_s); v_s.append(vn)
        x = _residual_matmul(mix, w_out_b, x, l, BN_OUT, "out_proj")
        h = _ffn_up(x, g_ffn, w_gate_b, w_up_b, l)
        x = _residual_matmul(h, w_down_b, x, l, BN_DOWN, "ffn_down")

    y_prompt, y_sample = _final_norm(x, final_norm_g.reshape(1, D_MODEL))
    return (y_prompt, y_sample, jnp.stack(ca_p), jnp.stack(ca_s), jnp.stack(pl_p), jnp.stack(pl_s),
            jnp.stack(sc_p), jnp.stack(sc_s), jnp.stack(v_s))
```

```python
import functools

import jax
import jax.numpy as jnp
from jax import lax
from jax.experimental import pallas as pl
from jax.experimental.pallas import tpu as pltpu

D_MODEL = 2048
BATCH = 4
SEQ = 2048
DEPTH = 4
DEC_BATCH = 128
DEC_SEQ = 4
PAST_LEN = 16384
D_GROUP = 512
N_SUB = 4
D_SUB = 128
N_PROJ = 8
D_PROJ = N_PROJ * D_GROUP
CONV_A_W = 31
POOL_WINDOWS = (2, 4, 8, 16)
POOL_BUF = 15
SCONV_W = 3
CHUNK = 128
D_FF = 5632
EPS = 1e-6

M_P = BATCH * SEQ
M_S = DEC_BATCH * DEC_SEQ
M_ALL = M_P + M_S

F32 = jnp.float32
BF16 = jnp.bfloat16

V7X_VMEM_LIMIT_BYTES = 56 * 1024 * 1024

BM = 1088
BM_OUT = 544
BN_IN = 1024
BN_FF = 512
BN_DOWN = 512
CAST_ROWS = 256
PREP_ROWS = 512

TS = 256
RC = 64
HALO_A = 32
HALO_P = 16
HALO_C = 8

BB = 32


def _params(sem, vmem=V7X_VMEM_LIMIT_BYTES):
    return pltpu.CompilerParams(dimension_semantics=sem, vmem_limit_bytes=vmem)


def _cast_rows_to_bf16(w_ref, wb_ref):
    def body(i, carry):
        r = pl.multiple_of(i * CAST_ROWS, CAST_ROWS)
        wb_ref[pl.ds(r, CAST_ROWS), :] = w_ref[pl.ds(r, CAST_ROWS), :].astype(BF16)
        return carry

    lax.fori_loop(0, w_ref.shape[0] // CAST_ROWS, body, 0)


def _row_scale(ss_ref):
    return lax.rsqrt(ss_ref[...] * (1.0 / D_MODEL) + EPS)


def _prep_kernel(xp_ref, xt_ref, g_ref, x_ref, xs_ref, ss_ref):
    def emit(x):
        x_ref[...] = x
        xs_ref[...] = (x * g_ref[...]).astype(BF16)
        ss_ref[...] = jnp.sum(x * x, axis=-1, keepdims=True)

    i = pl.program_id(0)

    @pl.when(i < M_P // PREP_ROWS)
    def _():
        emit(xp_ref[...])

    @pl.when(i >= M_P // PREP_ROWS)
    def _():
        emit(xt_ref[...])


def _prep(x_prompt, x_sample_tm, g):
    n_p = M_P // PREP_ROWS
    rows = pl.BlockSpec((PREP_ROWS, D_MODEL), lambda i: (i, 0))
    return pl.pallas_call(
        _prep_kernel,
        grid=(M_ALL // PREP_ROWS,),
        in_specs=[
            pl.BlockSpec((PREP_ROWS, D_MODEL), lambda i: (jnp.minimum(i, n_p - 1), 0)),
            pl.BlockSpec((PREP_ROWS, D_MODEL), lambda i: (jnp.maximum(i - n_p, 0), 0)),
            pl.BlockSpec((None, 1, D_MODEL), lambda i: (0, 0, 0)),
        ],
        out_specs=[rows, rows, pl.BlockSpec((PREP_ROWS, 1), lambda i: (i, 0))],
        out_shape=[
            jax.ShapeDtypeStruct((M_ALL, D_MODEL), F32),
            jax.ShapeDtypeStruct((M_ALL, D_MODEL), BF16),
            jax.ShapeDtypeStruct((M_ALL, 1), F32),
        ],
        compiler_params=_params(("arbitrary",)),
        name="prep",
    )(x_prompt, x_sample_tm, g)


def _in_proj_kernel(xs_ref, ss_ref, w_ref, o_ref, wb_ref):
    @pl.when(pl.program_id(1) == 0)
    def _():
        _cast_rows_to_bf16(w_ref, wb_ref)

    o_ref[...] = jnp.dot(xs_ref[...], wb_ref[...], preferred_element_type=F32) * _row_scale(ss_ref)


def _in_proj(xs, ss, w, layer):
    return pl.pallas_call(
        _in_proj_kernel,
        grid=(D_PROJ // BN_IN, M_ALL // BM),
        in_specs=[
            pl.BlockSpec((BM, D_MODEL), lambda j, i: (i, 0)),
            pl.BlockSpec((BM, 1), lambda j, i: (i, 0)),
            pl.BlockSpec((None, D_MODEL, BN_IN), lambda j, i: (layer, 0, j)),
        ],
        out_specs=pl.BlockSpec((BM, BN_IN), lambda j, i: (i, j)),
        out_shape=jax.ShapeDtypeStruct((M_ALL, D_PROJ), F32),
        scratch_shapes=[pltpu.VMEM((D_MODEL, BN_IN), BF16)],
        compiler_params=_params(("arbitrary", "arbitrary")),
        name=f"in_proj_l{layer}",
    )(xs, ss, w)


def _ffn_up_kernel(xs_ref, ss_ref, wg_ref, wu_ref, o_ref, wgb_ref, wub_ref):
    @pl.when(pl.program_id(1) == 0)
    def _():
        _cast_rows_to_bf16(wg_ref, wgb_ref)
        _cast_rows_to_bf16(wu_ref, wub_ref)

    xs = xs_ref[...]
    scale = _row_scale(ss_ref)
    gate = jnp.dot(xs, wgb_ref[...], preferred_element_type=F32) * scale
    up = jnp.dot(xs, wub_ref[...], preferred_element_type=F32) * scale
    o_ref[...] = (jax.nn.silu(gate) * up).astype(BF16)


def _ffn_up(xs, ss, wg, wu, layer):
    return pl.pallas_call(
        _ffn_up_kernel,
        grid=(D_FF // BN_FF, M_ALL // BM),
        in_specs=[
            pl.BlockSpec((BM, D_MODEL), lambda j, i: (i, 0)),
            pl.BlockSpec((BM, 1), lambda j, i: (i, 0)),
            pl.BlockSpec((None, D_MODEL, BN_FF), lambda j, i: (layer, 0, j)),
            pl.BlockSpec((None, D_MODEL, BN_FF), lambda j, i: (layer, 0, j)),
        ],
        out_specs=pl.BlockSpec((BM, BN_FF), lambda j, i: (i, j)),
        out_shape=jax.ShapeDtypeStruct((M_ALL, D_FF), BF16),
        scratch_shapes=[pltpu.VMEM((D_MODEL, BN_FF), BF16), pltpu.VMEM((D_MODEL, BN_FF), BF16)],
        compiler_params=_params(("arbitrary", "arbitrary")),
        name=f"ffn_up_l{layer}",
    )(xs, ss, wg, wu)


def _out_proj_kernel(a_ref, w_ref, r_ref, g_ref, x_ref, xs_ref, ss_ref, wb_ref):
    @pl.when(pl.program_id(0) == 0)
    def _():
        _cast_rows_to_bf16(w_ref, wb_ref)

    o = r_ref[...] + jnp.dot(a_ref[...], wb_ref[...], preferred_element_type=F32)
    x_ref[...] = o
    xs_ref[...] = (o * g_ref[...]).astype(BF16)
    ss_ref[...] = jnp.sum(o * o, axis=-1, keepdims=True)


def _out_proj(mix, w, res, g_next, layer):
    rows = pl.BlockSpec((BM_OUT, D_MODEL), lambda i: (i, 0))
    return pl.pallas_call(
        _out_proj_kernel,
        grid=(M_ALL // BM_OUT,),
        in_specs=[
            rows,
            pl.BlockSpec((None, D_MODEL, D_MODEL), lambda i: (layer, 0, 0), pipeline_mode=pl.Buffered(1)),
            rows,
            pl.BlockSpec((None, 1, D_MODEL), lambda i: (layer, 0, 0)),
        ],
        out_specs=[rows, rows, pl.BlockSpec((BM_OUT, 1), lambda i: (i, 0))],
        out_shape=[
            jax.ShapeDtypeStruct((M_ALL, D_MODEL), F32),
            jax.ShapeDtypeStruct((M_ALL, D_MODEL), BF16),
            jax.ShapeDtypeStruct((M_ALL, 1), F32),
        ],
        scratch_shapes=[pltpu.VMEM((D_MODEL, D_MODEL), BF16)],
        compiler_params=_params(("arbitrary",)),
        name=f"out_proj_l{layer}",
    )(mix, w, res, g_next)


def _ffn_down_kernel(h_ref, w_ref, r_ref, g_ref, x_ref, *rest, with_xs):
    ss_ref = rest[-1]
    o = r_ref[...] + jnp.dot(h_ref[...], w_ref[...], preferred_element_type=F32)
    x_ref[...] = o
    if with_xs:
        rest[0][...] = (o * g_ref[...]).astype(BF16)
    part = jnp.sum(o * o, axis=-1, keepdims=True)
    j = pl.program_id(1)

    @pl.when(j == 0)
    def _():
        ss_ref[...] = part

    @pl.when(j > 0)
    def _():
        ss_ref[...] += part


def _ffn_down(h, w, res, g_next, layer, with_xs):
    g_layer = layer + 1 if with_xs else layer
    tile = pl.BlockSpec((BM, BN_DOWN), lambda i, j: (i, j))
    ss_spec = pl.BlockSpec((BM, 1), lambda i, j: (i, 0))
    x_shape = jax.ShapeDtypeStruct((M_ALL, D_MODEL), F32)
    xs_shape = jax.ShapeDtypeStruct((M_ALL, D_MODEL), BF16)
    ss_shape = jax.ShapeDtypeStruct((M_ALL, 1), F32)
    outs = pl.pallas_call(
        functools.partial(_ffn_down_kernel, with_xs=with_xs),
        grid=(M_ALL // BM, D_MODEL // BN_DOWN),
        in_specs=[
            pl.BlockSpec((BM, D_FF), lambda i, j: (i, 0)),
            pl.BlockSpec((None, D_FF, BN_DOWN), lambda i, j: (layer, 0, j)),
            tile,
            pl.BlockSpec((None, 1, BN_DOWN), lambda i, j: (g_layer, 0, j)),
        ],
        out_specs=[tile, tile, ss_spec] if with_xs else [tile, ss_spec],
        out_shape=[x_shape, xs_shape, ss_shape] if with_xs else [x_shape, ss_shape],
        compiler_params=_params(("arbitrary", "arbitrary")),
        name=f"ffn_down_l{layer}",
    )(h, w, res, g_next)
    return outs if with_xs else (outs[0], None, outs[1])


def _layernorm(x, g, b):
    mu = jnp.mean(x, axis=-1, keepdims=True)
    xc = x - mu
    var = jnp.mean(xc * xc, axis=-1, keepdims=True)
    return xc * lax.rsqrt(var + EPS) * g + b


def _rms(x, g):
    return x * lax.rsqrt(jnp.mean(x * x, axis=-1, keepdims=True) + EPS) * g


def _group(g):
    return slice(g * D_SUB, (g + 1) * D_SUB)


def _mixer(m):
    return slice(m * D_GROUP, (m + 1) * D_GROUP)


def _mixer_prompt_kernel(proj_ref, caw_ref, cab_ref, lag_ref, lab_ref, pw_ref, ps_ref, scw_ref,
                         sg_ref, sb_ref, sw_ref, sbias_ref, ong_ref,
                         mix_ref, sta_ref, stp_ref, stc_ref,
                         gbuf, pbuf, cbuf, vbuf):
    s = pl.program_id(1)

    @pl.when(s == 0)
    def _():
        gbuf[0:HALO_A, :] = jnp.zeros((HALO_A, D_GROUP), F32)
        pbuf[0:HALO_P, :] = jnp.zeros((HALO_P, D_GROUP), F32)
        cbuf[0:HALO_C, :] = jnp.zeros((HALO_C, D_GROUP), F32)

    @pl.when(s > 0)
    def _():
        gbuf[0:HALO_A, :] = gbuf[TS:TS + HALO_A, :]
        pbuf[0:HALO_P, :] = pbuf[TS:TS + HALO_P, :]
        cbuf[0:HALO_C, :] = cbuf[TS:TS + HALO_C, :]

    sg = sg_ref[...]
    sb = sb_ref[...]
    for c in range(TS // RC):
        r0 = c * RC

        def col(k, r0=r0):
            return proj_ref[r0:r0 + RC, k * D_GROUP:(k + 1) * D_GROUP]

        gbuf[HALO_A + r0:HALO_A + r0 + RC, :] = col(0) * jax.nn.sigmoid(col(1))
        pbuf[HALO_P + r0:HALO_P + r0 + RC, :] = col(2)
        cbuf[HALO_C + r0:HALO_C + r0 + RC, :] = col(5) * col(3)
        vbuf[r0:r0 + RC, :] = _layernorm(col(7), sg, sb)

    cab = cab_ref[...]
    lag = lag_ref[...]
    lab = lab_ref[...]
    ps = ps_ref[...]
    for c in range(TS // RC):
        r0 = c * RC

        acc = None
        for r in range(8):
            part = None
            for q in range(4):
                wi = CONV_A_W - 1 - 8 * q - r
                if wi < 0:
                    continue
                lo = r0 + HALO_A - 8 - 8 * q
                term = caw_ref[wi:wi + 1, :] * gbuf[lo:lo + RC + 8, :]
                part = term if part is None else part + term
            if r:
                part = pltpu.roll(part, r, axis=0)
            part = part[8:8 + RC, :]
            acc = part if acc is None else acc + part
        ya = _layernorm(acc + cab, lag, lab)
        ya = ya * jax.nn.sigmoid(ya)
        mix_ref[r0:r0 + RC, _mixer(0)] = _rms(ya, ong_ref[:, _mixer(0)]).astype(BF16)

        pos = s * TS + r0 + lax.broadcasted_iota(jnp.int32, (RC, D_SUB), 0)
        yb = []
        for g, w in enumerate(POOL_WINDOWS):
            hist = pbuf[r0:r0 + RC + HALO_P, _group(g)]
            tot = hist
            span = 1
            while span < w:
                tot = tot + pltpu.roll(tot, span, axis=0)
                span *= 2
            cnt = jnp.minimum(pos + 1, w).astype(F32)
            pooled = tot[HALO_P:, :] / cnt - hist[HALO_P:, :]
            yb.append(jnp.dot(pooled.astype(BF16), pw_ref[g].astype(BF16), preferred_element_type=F32))
        yb = jnp.concatenate(yb, axis=-1) * ps
        mix_ref[r0:r0 + RC, _mixer(1)] = _rms(yb, ong_ref[:, _mixer(1)]).astype(BF16)

        hist = cbuf[r0:r0 + RC + HALO_C, :]
        conv = (scw_ref[2:3, :] * hist + scw_ref[1:2, :] * pltpu.roll(hist, 1, axis=0)
                + scw_ref[0:1, :] * pltpu.roll(hist, 2, axis=0))
        yc = proj_ref[r0:r0 + RC, 4 * D_GROUP:5 * D_GROUP] * conv[HALO_C:, :]
        mix_ref[r0:r0 + RC, _mixer(2)] = _rms(yc, ong_ref[:, _mixer(2)]).astype(BF16)

    row = lax.broadcasted_iota(jnp.int32, (CHUNK, CHUNK), 0)
    colid = lax.broadcasted_iota(jnp.int32, (CHUNK, CHUNK), 1)
    tril = (row >= colid).astype(F32)
    wm = [(sw_ref[g] * tril).astype(BF16) for g in range(N_SUB)]
    for c in range(TS // CHUNK):
        r0 = c * CHUNK
        v = vbuf[r0:r0 + CHUNK, :].astype(BF16)
        z = [jnp.dot(wm[g], v[:, _group(g)], preferred_element_type=F32) + sbias_ref[:, g:g + 1]
             for g in range(N_SUB)]
        yd = proj_ref[r0:r0 + CHUNK, 6 * D_GROUP:7 * D_GROUP] * jnp.concatenate(z, axis=-1)
        mix_ref[r0:r0 + CHUNK, _mixer(3)] = _rms(yd, ong_ref[:, _mixer(3)]).astype(BF16)

    @pl.when(s == pl.num_programs(1) - 1)
    def _():
        sta_ref[...] = pltpu.roll(gbuf[TS:TS + HALO_A, :], CONV_A_W - 1, axis=0)[0:CONV_A_W - 1, :]
        stp_ref[...] = pltpu.roll(pbuf[TS:TS + HALO_P, :], POOL_BUF, axis=0)[0:POOL_BUF, :]
        stc_ref[...] = pltpu.roll(cbuf[TS:TS + HALO_C, :], SCONV_W - 1, axis=0)[0:SCONV_W - 1, :]


def _mixer_prompt(proj, p, layer):
    n_s = SEQ // TS

    def lp(shape):
        nd = len(shape)
        return pl.BlockSpec((None,) + shape, lambda b, s: (layer,) + (0,) * nd)

    state = lambda n: pl.BlockSpec((None, n, D_GROUP), lambda b, s: (b, 0, 0))
    return pl.pallas_call(
        _mixer_prompt_kernel,
        grid=(BATCH, n_s),
        in_specs=[
            pl.BlockSpec((TS, D_PROJ), lambda b, s: (b * n_s + s, 0)),
            lp((CONV_A_W, D_GROUP)), lp((1, D_GROUP)), lp((1, D_GROUP)), lp((1, D_GROUP)),
            lp((N_SUB, D_SUB, D_SUB)), lp((1, D_GROUP)), lp((SCONV_W, D_GROUP)),
            lp((1, D_GROUP)), lp((1, D_GROUP)), lp((N_SUB, CHUNK, CHUNK)), lp((CHUNK, N_SUB)),
            lp((1, D_MODEL)),
        ],
        out_specs=[
            pl.BlockSpec((TS, D_MODEL), lambda b, s: (b * n_s + s, 0)),
            state(CONV_A_W - 1), state(POOL_BUF), state(SCONV_W - 1),
        ],
        out_shape=[
            jax.ShapeDtypeStruct((M_ALL, D_MODEL), BF16),
            jax.ShapeDtypeStruct((BATCH, CONV_A_W - 1, D_GROUP), F32),
            jax.ShapeDtypeStruct((BATCH, POOL_BUF, D_GROUP), F32),
            jax.ShapeDtypeStruct((BATCH, SCONV_W - 1, D_GROUP), F32),
        ],
        scratch_shapes=[
            pltpu.VMEM((HALO_A + TS, D_GROUP), F32),
            pltpu.VMEM((HALO_P + TS, D_GROUP), F32),
            pltpu.VMEM((HALO_C + TS, D_GROUP), F32),
            pltpu.VMEM((TS, D_GROUP), F32),
        ],
        compiler_params=_params(("arbitrary", "arbitrary")),
        name=f"mixer_prompt_l{layer}",
    )(proj, p["conv_a_w"], p["conv_a_b"], p["ln_a_g"], p["ln_a_b"], p["pool_w"], p["pool_scale"],
      p["sconv_w"], p["sgu_ln_g"], p["sgu_ln_b"], p["sgu_w"], p["sgu_b_t"], p["out_norm_g"])


N_SAMPLE_IN = 16
N_SAMPLE_OUT = 5


def _mixer_sample_kernel(*refs, layer):
    (wsm_ref, bsm_ref, proj_ref, sa_ref, sp_ref, sc_ref, caw_ref, cab_ref, lag_ref, lab_ref, pw_ref,
     ps_ref, scw_ref, sg_ref, sb_ref, ong_ref) = refs[:N_SAMPLE_IN]
    mix_ref, nsa_ref, nsp_ref, nsc_ref, vn_ref = refs[-N_SAMPLE_OUT:]

    def col(t, k):
        return proj_ref[t, :, k * D_GROUP:(k + 1) * D_GROUP]

    def slab(j):
        return slice(j * D_GROUP, (j + 1) * D_GROUP)

    steps = range(DEC_SEQ)

    glu = [col(t, 0) * jax.nn.sigmoid(col(t, 1)) for t in steps]
    n_a = CONV_A_W - 1

    def za(j):
        return sa_ref[:, slab(j)] if j < n_a else glu[j - n_a]

    cab = cab_ref[...]
    lag = lag_ref[...]
    lab = lab_ref[...]
    for t in steps:
        acc = caw_ref[0:1, :] * za(t)
        for k in range(1, CONV_A_W):
            acc = acc + caw_ref[k:k + 1, :] * za(t + k)
        ya = _layernorm(acc + cab, lag, lab)
        ya = ya * jax.nn.sigmoid(ya)
        mix_ref[t, :, _mixer(0)] = _rms(ya, ong_ref[:, _mixer(0)]).astype(BF16)
    for j in range(n_a):
        nsa_ref[:, slab(j)] = za(j + DEC_SEQ)

    pin = [col(t, 2) for t in steps]

    def zp(j):
        return sp_ref[:, slab(j)] if j < POOL_BUF else pin[j - POOL_BUF]

    ps = ps_ref[...]
    for t in steps:
        yb = []
        for g, w in enumerate(POOL_WINDOWS):
            cnt = float(min(PAST_LEN + t + 1, w))
            tot = zp(POOL_BUF + t)[:, _group(g)]
            for d in range(1, w):
                tot = tot + zp(POOL_BUF + t - d)[:, _group(g)]
            pooled = tot / cnt - pin[t][:, _group(g)]
            yb.append(jnp.dot(pooled.astype(BF16), pw_ref[g].astype(BF16), preferred_element_type=F32))
        yb = jnp.concatenate(yb, axis=-1) * ps
        mix_ref[t, :, _mixer(1)] = _rms(yb, ong_ref[:, _mixer(1)]).astype(BF16)
    for j in range(POOL_BUF):
        nsp_ref[:, slab(j)] = zp(j + DEC_SEQ)

    gated = [col(t, 5) * col(t, 3) for t in steps]
    n_c = SCONV_W - 1

    def zc(j):
        return sc_ref[:, slab(j)] if j < n_c else gated[j - n_c]

    for t in steps:
        conv = scw_ref[0:1, :] * zc(t) + scw_ref[1:2, :] * zc(t + 1) + scw_ref[2:3, :] * zc(t + 2)
        yc = col(t, 4) * conv
        mix_ref[t, :, _mixer(2)] = _rms(yc, ong_ref[:, _mixer(2)]).astype(BF16)
    for j in range(n_c):
        nsc_ref[:, slab(j)] = zc(j + DEC_SEQ)

    sg = sg_ref[...]
    sb = sb_ref[...]
    vn = [_layernorm(col(t, 7), sg, sb) for t in steps]
    for t in steps:
        vn_ref[:, slab(t)] = vn[t]
    for t in steps:
        z = []
        for g in range(N_SUB):
            base = (layer * N_SUB + g) * DEC_SEQ * DEC_SEQ + t * DEC_SEQ
            zg = wsm_ref[base] * vn[0][:, _group(g)]
            for j in range(1, t + 1):
                zg = zg + wsm_ref[base + j] * vn[j][:, _group(g)]
            z.append(zg + bsm_ref[(layer * N_SUB + g) * DEC_SEQ + t])
        yd = col(t, 6) * jnp.concatenate(z, axis=-1)
        mix_ref[t, :, _mixer(3)] = _rms(yd, ong_ref[:, _mixer(3)]).astype(BF16)


def _mixer_sample(proj, mix, states, stacked, p, layer):
    sa, sp, sc = states
    slabs = M_ALL // DEC_BATCH
    first = M_P // DEC_BATCH // DEC_SEQ
    proj3 = proj.reshape(slabs, DEC_BATCH, D_PROJ)
    mix3 = mix.reshape(slabs, DEC_BATCH, D_MODEL)
    in_place = (mix3,) + tuple(stacked)

    def lp(shape):
        nd = len(shape)
        return pl.BlockSpec((None,) + shape, lambda i: (layer,) + (0,) * nd)

    smem = pl.BlockSpec(memory_space=pltpu.SMEM)
    st = lambda n: pl.BlockSpec((None, BB, n * D_GROUP), lambda i: (layer, i, 0))
    st_shape = lambda n: jax.ShapeDtypeStruct((DEPTH, DEC_BATCH, n * D_GROUP), F32)
    n_a, n_c = CONV_A_W - 1, SCONV_W - 1
    outs = pl.pallas_call(
        functools.partial(_mixer_sample_kernel, layer=layer),
        grid=(DEC_BATCH // BB,),
        in_specs=[
            smem, smem,
            pl.BlockSpec((DEC_SEQ, BB, D_PROJ), lambda i: (first, i, 0)),
            st(n_a), st(POOL_BUF), st(n_c),
            lp((CONV_A_W, D_GROUP)), lp((1, D_GROUP)), lp((1, D_GROUP)), lp((1, D_GROUP)),
            lp((N_SUB, D_SUB, D_SUB)), lp((1, D_GROUP)), lp((SCONV_W, D_GROUP)),
            lp((1, D_GROUP)), lp((1, D_GROUP)), lp((1, D_MODEL)),
        ] + [pl.BlockSpec(memory_space=pl.ANY)] * len(in_place),
        out_specs=[
            pl.BlockSpec((DEC_SEQ, BB, D_MODEL), lambda i: (first, i, 0)),
            st(n_a), st(POOL_BUF), st(n_c), st(DEC_SEQ),
        ],
        out_shape=[
            jax.ShapeDtypeStruct((slabs, DEC_BATCH, D_MODEL), BF16),
            st_shape(n_a), st_shape(POOL_BUF), st_shape(n_c), st_shape(DEC_SEQ),
        ],
        input_output_aliases={N_SAMPLE_IN + k: k for k in range(len(in_place))},
        compiler_params=_params(("arbitrary",)),
        name=f"mixer_sample_l{layer}",
    )(p["sgu_w_small"], p["sgu_b_small"], proj3, sa, sp, sc,
      p["conv_a_w"], p["conv_a_b"], p["ln_a_g"], p["ln_a_b"], p["pool_w"], p["pool_scale"],
      p["sconv_w"], p["sgu_ln_g"], p["sgu_ln_b"], p["out_norm_g"], *in_place)
    return outs[0].reshape(M_ALL, D_MODEL), tuple(outs[1:])


FINAL_ROWS = 512


def _final_norm_kernel(x_ref, ss_ref, g_ref, o_ref):
    o_ref[...] = x_ref[...] * _row_scale(ss_ref) * g_ref[...]


def _final_norm(x, ss, g):
    def call(n_rows, first_block, name):
        rows = lambda c: pl.BlockSpec((FINAL_ROWS, c), lambda i: (first_block + i, 0))
        return pl.pallas_call(
            _final_norm_kernel,
            grid=(n_rows // FINAL_ROWS,),
            in_specs=[rows(D_MODEL), rows(1), pl.BlockSpec((1, D_MODEL), lambda i: (0, 0))],
            out_specs=pl.BlockSpec((FINAL_ROWS, D_MODEL), lambda i: (i, 0)),
            out_shape=jax.ShapeDtypeStruct((n_rows, D_MODEL), F32),
            compiler_params=_params(("arbitrary",)),
            name=name,
        )(x, ss, g)

    y_p = call(M_P, 0, "final_norm_prompt")
    y_s = call(M_S, M_P // FINAL_ROWS, "final_norm_sample")
    return (y_p.reshape(BATCH, SEQ, D_MODEL),
            jnp.swapaxes(y_s.reshape(DEC_SEQ, DEC_BATCH, D_MODEL), 0, 1))


def kernel(x_prompt, x_sample, state_conv_a, state_pool, state_sconv, norm_mix_g, w_in, conv_a_w, conv_a_b, ln_a_g, ln_a_b, pool_w, pool_scale, sconv_w, sgu_ln_g, sgu_ln_b, sgu_w, sgu_b, out_norm_g, w_out, norm_ffn_g, w_gate, w_up, w_down, final_norm_g):
    row = lambda a: a.reshape(DEPTH, 1, a.shape[-1])
    p = {
        "conv_a_w": conv_a_w, "conv_a_b": row(conv_a_b), "ln_a_g": row(ln_a_g), "ln_a_b": row(ln_a_b),
        "pool_w": pool_w, "pool_scale": row(pool_scale), "sconv_w": sconv_w,
        "sgu_ln_g": row(sgu_ln_g), "sgu_ln_b": row(sgu_ln_b), "sgu_w": sgu_w,
        "sgu_b_t": jnp.swapaxes(sgu_b, 1, 2),
        "sgu_w_small": sgu_w[:, :, :DEC_SEQ, :DEC_SEQ].reshape(-1),
        "sgu_b_small": sgu_b[:, :, :DEC_SEQ].reshape(-1),
        "out_norm_g": row(out_norm_g),
    }
    g_mix = row(norm_mix_g)
    g_ffn = row(norm_ffn_g)
    w_down_b = w_down.astype(BF16)

    states = (state_conv_a.reshape(DEPTH, DEC_BATCH, -1), state_pool.reshape(DEPTH, DEC_BATCH, -1),
              state_sconv.reshape(DEPTH, DEC_BATCH, -1))

    x, xs, ss = _prep(x_prompt.reshape(M_P, D_MODEL),
                      jnp.swapaxes(x_sample, 0, 1).reshape(M_S, D_MODEL), g_mix)

    ca_p, pl_p, sc_p = [], [], []
    stacked = ()
    for l in range(DEPTH):
        last = l == DEPTH - 1
        proj = _in_proj(xs, ss, w_in, l)
        mix, a_p, p_p, c_p = _mixer_prompt(proj, p, l)
        mix, stacked = _mixer_sample(proj, mix, states, stacked, p, l)
        ca_p.append(a_p); pl_p.append(p_p); sc_p.append(c_p)
        x, xs, ss = _out_proj(mix, w_out, x, g_ffn, l)
        h = _ffn_up(xs, ss, w_gate, w_up, l)
        x, xs, ss = _ffn_down(h, w_down_b, x, g_mix, l, not last)

    y_prompt, y_sample = _final_norm(x, ss, final_norm_g.reshape(1, D_MODEL))
    ca_s, pl_s, sc_s, v_s = stacked
    return (y_prompt, y_sample, jnp.stack(ca_p),
            ca_s.reshape(DEPTH, DEC_BATCH, CONV_A_W - 1, D_GROUP), jnp.stack(pl_p),
            pl_s.reshape(DEPTH, DEC_BATCH, POOL_BUF, D_GROUP), jnp.stack(sc_p),
            sc_s.reshape(DEPTH, DEC_BATCH, SCONV_W - 1, D_GROUP),
            v_s.reshape(DEPTH, DEC_BATCH, DEC_SEQ, D_GROUP))
```

```python
import functools

import jax
import jax.numpy as jnp
from jax import lax
from jax.experimental import pallas as pl
from jax.experimental.pallas import tpu as pltpu

D_MODEL = 2048
BATCH = 4
SEQ = 2048
DEPTH = 4
DEC_BATCH = 128
DEC_SEQ = 4
PAST_LEN = 16384
D_GROUP = 512
N_SUB = 4
D_SUB = 128
N_PROJ = 8
D_PROJ = N_PROJ * D_GROUP
CONV_A_W = 31
POOL_WINDOWS = (2, 4, 8, 16)
POOL_BUF = 15
SCONV_W = 3
CHUNK = 128
D_FF = 5632
EPS = 1e-6

M_P = BATCH * SEQ
M_S = DEC_BATCH * DEC_SEQ
M_ALL = M_P + M_S

F32 = jnp.float32
BF16 = jnp.bfloat16

V7X_VMEM_LIMIT_BYTES = 56 * 1024 * 1024

BM = 1088
BM_OUT = 544
BN_IN = 1024
BN_FF = 512
BN_DOWN = 512
CAST_ROWS = 256
PREP_ROWS = 512

BM_FF = 1088

TS = 512
RC = 64
HALO_A = 32
HALO_P = 16
HALO_C = 8

BB = 32


def _params(sem, vmem=V7X_VMEM_LIMIT_BYTES):
    return pltpu.CompilerParams(dimension_semantics=sem, vmem_limit_bytes=vmem)


def _cast_rows_to_bf16(w_ref, wb_ref):
    def body(i, carry):
        r = pl.multiple_of(i * CAST_ROWS, CAST_ROWS)
        wb_ref[pl.ds(r, CAST_ROWS), :] = w_ref[pl.ds(r, CAST_ROWS), :].astype(BF16)
        return carry

    lax.fori_loop(0, w_ref.shape[0] // CAST_ROWS, body, 0)


def _row_scale(ss_ref):
    return lax.rsqrt(ss_ref[...] * (1.0 / D_MODEL) + EPS)


def _prep_kernel(xp_ref, xt_ref, g_ref, x_ref, xs_ref, ss_ref):
    def emit(x):
        x_ref[...] = x
        xs_ref[...] = (x * g_ref[...]).astype(BF16)
        ss_ref[...] = jnp.sum(x * x, axis=-1, keepdims=True)

    i = pl.program_id(0)

    @pl.when(i < M_P // PREP_ROWS)
    def _():
        emit(xp_ref[...])

    @pl.when(i >= M_P // PREP_ROWS)
    def _():
        emit(xt_ref[...])


def _prep(x_prompt, x_sample_tm, g):
    n_p = M_P // PREP_ROWS
    rows = pl.BlockSpec((PREP_ROWS, D_MODEL), lambda i: (i, 0))
    return pl.pallas_call(
        _prep_kernel,
        grid=(M_ALL // PREP_ROWS,),
        in_specs=[
            pl.BlockSpec((PREP_ROWS, D_MODEL), lambda i: (jnp.minimum(i, n_p - 1), 0)),
            pl.BlockSpec((PREP_ROWS, D_MODEL), lambda i: (jnp.maximum(i - n_p, 0), 0)),
            pl.BlockSpec((None, 1, D_MODEL), lambda i: (0, 0, 0)),
        ],
        out_specs=[rows, rows, pl.BlockSpec((PREP_ROWS, 1), lambda i: (i, 0))],
        out_shape=[
            jax.ShapeDtypeStruct((M_ALL, D_MODEL), F32),
            jax.ShapeDtypeStruct((M_ALL, D_MODEL), BF16),
            jax.ShapeDtypeStruct((M_ALL, 1), F32),
        ],
        compiler_params=_params(("arbitrary",)),
        name="prep",
    )(x_prompt, x_sample_tm, g)


def _in_proj_sample_kernel(xs_ref, ss_ref, w_ref, o_ref):
    o_ref[...] = jnp.dot(xs_ref[...], w_ref[...], preferred_element_type=F32) * _row_scale(ss_ref)


def _in_proj_sample(xs, ss, w, layer):
    first = M_P // M_S
    return pl.pallas_call(
        _in_proj_sample_kernel,
        grid=(D_PROJ // BN_IN,),
        in_specs=[
            pl.BlockSpec((M_S, D_MODEL), lambda j: (first, 0)),
            pl.BlockSpec((M_S, 1), lambda j: (first, 0)),
            pl.BlockSpec((None, D_MODEL, BN_IN), lambda j: (layer, 0, j)),
        ],
        out_specs=pl.BlockSpec((M_S, BN_IN), lambda j: (0, j)),
        out_shape=jax.ShapeDtypeStruct((M_S, D_PROJ), F32),
        compiler_params=_params(("arbitrary",)),
        name=f"in_proj_sample_l{layer}",
    )(xs, ss, w)


def _ffn_up_kernel(xs_ref, ss_ref, wg_ref, wu_ref, o_ref, wgb_ref, wub_ref):
    @pl.when(pl.program_id(1) == 0)
    def _():
        _cast_rows_to_bf16(wg_ref, wgb_ref)
        _cast_rows_to_bf16(wu_ref, wub_ref)

    xs = xs_ref[...]
    scale = _row_scale(ss_ref)
    gate = jnp.dot(xs, wgb_ref[...], preferred_element_type=F32) * scale
    up = jnp.dot(xs, wub_ref[...], preferred_element_type=F32) * scale
    o_ref[...] = (jax.nn.silu(gate) * up).astype(BF16)


def _ffn_up(xs, ss, wg, wu, layer):
    return pl.pallas_call(
        _ffn_up_kernel,
        grid=(D_FF // BN_FF, M_ALL // BM_FF),
        in_specs=[
            pl.BlockSpec((BM_FF, D_MODEL), lambda j, i: (i, 0)),
            pl.BlockSpec((BM_FF, 1), lambda j, i: (i, 0)),
            pl.BlockSpec((None, D_MODEL, BN_FF), lambda j, i: (layer, 0, j)),
            pl.BlockSpec((None, D_MODEL, BN_FF), lambda j, i: (layer, 0, j)),
        ],
        out_specs=pl.BlockSpec((BM_FF, BN_FF), lambda j, i: (i, j)),
        out_shape=jax.ShapeDtypeStruct((M_ALL, D_FF), BF16),
        scratch_shapes=[pltpu.VMEM((D_MODEL, BN_FF), BF16), pltpu.VMEM((D_MODEL, BN_FF), BF16)],
        compiler_params=_params(("arbitrary", "arbitrary")),
        name=f"ffn_up_l{layer}",
    )(xs, ss, wg, wu)


def _out_proj_kernel(a_ref, w_ref, r_ref, g_ref, x_ref, xs_ref, ss_ref, wb_ref):
    @pl.when(pl.program_id(0) == 0)
    def _():
        _cast_rows_to_bf16(w_ref, wb_ref)

    o = r_ref[...] + jnp.dot(a_ref[...], wb_ref[...], preferred_element_type=F32)
    x_ref[...] = o
    xs_ref[...] = (o * g_ref[...]).astype(BF16)
    ss_ref[...] = jnp.sum(o * o, axis=-1, keepdims=True)


def _out_proj(mix, w, res, g_next, layer):
    rows = pl.BlockSpec((BM_OUT, D_MODEL), lambda i: (i, 0))
    return pl.pallas_call(
        _out_proj_kernel,
        grid=(M_ALL // BM_OUT,),
        in_specs=[
            rows,
            pl.BlockSpec((None, D_MODEL, D_MODEL), lambda i: (layer, 0, 0), pipeline_mode=pl.Buffered(1)),
            rows,
            pl.BlockSpec((None, 1, D_MODEL), lambda i: (layer, 0, 0)),
        ],
        out_specs=[rows, rows, pl.BlockSpec((BM_OUT, 1), lambda i: (i, 0))],
        out_shape=[
            jax.ShapeDtypeStruct((M_ALL, D_MODEL), F32),
            jax.ShapeDtypeStruct((M_ALL, D_MODEL), BF16),
            jax.ShapeDtypeStruct((M_ALL, 1), F32),
        ],
        scratch_shapes=[pltpu.VMEM((D_MODEL, D_MODEL), BF16)],
        compiler_params=_params(("arbitrary",)),
        name=f"out_proj_l{layer}",
    )(mix, w, res, g_next)


def _ffn_down_kernel(h_ref, w_ref, r_ref, g_ref, x_ref, *rest, with_xs):
    ss_ref = rest[-1]
    o = r_ref[...] + jnp.dot(h_ref[...], w_ref[...], preferred_element_type=F32)
    x_ref[...] = o
    if with_xs:
        rest[0][...] = (o * g_ref[...]).astype(BF16)
    part = jnp.sum(o * o, axis=-1, keepdims=True)
    j = pl.program_id(1)

    @pl.when(j == 0)
    def _():
        ss_ref[...] = part

    @pl.when(j > 0)
    def _():
        ss_ref[...] += part


def _ffn_down(h, w, res, g_next, layer, with_xs):
    g_layer = layer + 1 if with_xs else layer
    tile = pl.BlockSpec((BM, BN_DOWN), lambda i, j: (i, j))
    ss_spec = pl.BlockSpec((BM, 1), lambda i, j: (i, 0))
    x_shape = jax.ShapeDtypeStruct((M_ALL, D_MODEL), F32)
    xs_shape = jax.ShapeDtypeStruct((M_ALL, D_MODEL), BF16)
    ss_shape = jax.ShapeDtypeStruct((M_ALL, 1), F32)
    outs = pl.pallas_call(
        functools.partial(_ffn_down_kernel, with_xs=with_xs),
        grid=(M_ALL // BM, D_MODEL // BN_DOWN),
        in_specs=[
            pl.BlockSpec((BM, D_FF), lambda i, j: (i, 0)),
            pl.BlockSpec((None, D_FF, BN_DOWN), lambda i, j: (layer, 0, j)),
            tile,
            pl.BlockSpec((None, 1, BN_DOWN), lambda i, j: (g_layer, 0, j)),
        ],
        out_specs=[tile, tile, ss_spec] if with_xs else [tile, ss_spec],
        out_shape=[x_shape, xs_shape, ss_shape] if with_xs else [x_shape, ss_shape],
        compiler_params=_params(("arbitrary", "arbitrary")),
        name=f"ffn_down_l{layer}",
    )(h, w, res, g_next)
    return outs if with_xs else (outs[0], None, outs[1])


def _layernorm(x, g, b):
    mu = jnp.mean(x, axis=-1, keepdims=True)
    xc = x - mu
    var = jnp.mean(xc * xc, axis=-1, keepdims=True)
    return xc * lax.rsqrt(var + EPS) * g + b


def _rms(x, g):
    return x * lax.rsqrt(jnp.mean(x * x, axis=-1, keepdims=True) + EPS) * g


def _group(g):
    return slice(g * D_SUB, (g + 1) * D_SUB)


def _mixer(m):
    return slice(m * D_GROUP, (m + 1) * D_GROUP)


def _mixer_prompt_kernel(xs_ref, ss_ref, w_ref, caw_ref, cab_ref, lag_ref, lab_ref, pw_ref, ps_ref, scw_ref,
                         sg_ref, sb_ref, sw_ref, sbias_ref, ong_ref,
                         mix_ref, sta_ref, stp_ref, stc_ref,
                         gbuf, pbuf, cbuf, vbuf, *proj_refs):
    s = pl.program_id(1)

    @pl.when(s == 0)
    def _():
        gbuf[0:HALO_A, :] = jnp.zeros((HALO_A, D_GROUP), F32)
        pbuf[0:HALO_P, :] = jnp.zeros((HALO_P, D_GROUP), F32)
        cbuf[0:HALO_C, :] = jnp.zeros((HALO_C, D_GROUP), F32)

    @pl.when(s > 0)
    def _():
        gbuf[0:HALO_A, :] = gbuf[TS:TS + HALO_A, :]
        pbuf[0:HALO_P, :] = pbuf[TS:TS + HALO_P, :]
        cbuf[0:HALO_C, :] = cbuf[TS:TS + HALO_C, :]

    scale = _row_scale(ss_ref)
    sg = sg_ref[...]
    sb = sb_ref[...]
    cab = cab_ref[...]
    lag = lag_ref[...]
    lab = lab_ref[...]
    ps = ps_ref[...]
    rows = lambda c: slice(c * RC, (c + 1) * RC)

    def project(k):
        proj_refs[k][...] = jnp.dot(xs_ref[...], w_ref[:, k * D_GROUP:(k + 1) * D_GROUP],
                                    preferred_element_type=F32)

    def col(k, c, n=RC):
        r = slice(c * n, (c + 1) * n)
        return proj_refs[k][r, :] * scale[r, :]

    def stage_glu(c):
        gbuf[HALO_A + c * RC:HALO_A + (c + 1) * RC, :] = col(0, c) * jax.nn.sigmoid(col(1, c))

    def stage_pool(c):
        pbuf[HALO_P + c * RC:HALO_P + (c + 1) * RC, :] = col(2, c)

    def stage_gated(c):
        cbuf[HALO_C + c * RC:HALO_C + (c + 1) * RC, :] = col(5, c) * col(3, c)

    def stage_v(c):
        vbuf[rows(c), :] = _layernorm(col(7, c), sg, sb)

    def conv_module(c):
        r0 = c * RC
        acc = None
        for r in range(8):
            part = None
            for q in range(4):
                wi = CONV_A_W - 1 - 8 * q - r
                if wi < 0:
                    continue
                lo = r0 + HALO_A - 8 - 8 * q
                term = caw_ref[wi:wi + 1, :] * gbuf[lo:lo + RC + 8, :]
                part = term if part is None else part + term
            if r:
                part = pltpu.roll(part, r, axis=0)
            part = part[8:8 + RC, :]
            acc = part if acc is None else acc + part
        ya = _layernorm(acc + cab, lag, lab)
        ya = ya * jax.nn.sigmoid(ya)
        mix_ref[rows(c), _mixer(0)] = _rms(ya, ong_ref[:, _mixer(0)]).astype(BF16)

    def pooling(c):
        r0 = c * RC
        pos = s * TS + r0 + lax.broadcasted_iota(jnp.int32, (RC, D_SUB), 0)
        yb = []
        for g, w in enumerate(POOL_WINDOWS):
            hist = pbuf[r0:r0 + RC + HALO_P, _group(g)]
            tot = hist
            span = 1
            while span < w:
                tot = tot + pltpu.roll(tot, span, axis=0)
                span *= 2
            cnt = jnp.minimum(pos + 1, w).astype(F32)
            pooled = tot[HALO_P:, :] / cnt - hist[HALO_P:, :]
            yb.append(jnp.dot(pooled.astype(BF16), pw_ref[g].astype(BF16), preferred_element_type=F32))
        yb = jnp.concatenate(yb, axis=-1) * ps
        mix_ref[rows(c), _mixer(1)] = _rms(yb, ong_ref[:, _mixer(1)]).astype(BF16)

    def gated_conv(c):
        r0 = c * RC
        hist = cbuf[r0:r0 + RC + HALO_C, :]
        conv = (scw_ref[2:3, :] * hist + scw_ref[1:2, :] * pltpu.roll(hist, 1, axis=0)
                + scw_ref[0:1, :] * pltpu.roll(hist, 2, axis=0))
        yc = col(4, c) * conv[HALO_C:, :]
        mix_ref[rows(c), _mixer(2)] = _rms(yc, ong_ref[:, _mixer(2)]).astype(BF16)

    row = lax.broadcasted_iota(jnp.int32, (CHUNK, CHUNK), 0)
    colid = lax.broadcasted_iota(jnp.int32, (CHUNK, CHUNK), 1)
    tril = (row >= colid).astype(F32)

    def spatial_gate(c):
        r0 = c * CHUNK
        v = vbuf[r0:r0 + CHUNK, :].astype(BF16)
        z = [jnp.dot((sw_ref[g] * tril).astype(BF16), v[:, _group(g)], preferred_element_type=F32)
             + sbias_ref[:, g:g + 1] for g in range(N_SUB)]
        yd = col(6, c, CHUNK) * jnp.concatenate(z, axis=-1)
        mix_ref[r0:r0 + CHUNK, _mixer(3)] = _rms(yd, ong_ref[:, _mixer(3)]).astype(BF16)

    every = lambda f: [functools.partial(f, c) for c in range(TS // RC)]
    convs = every(conv_module)
    gates = [functools.partial(spatial_gate, c) for c in range(TS // CHUNK)]
    plan = [
        (7, []),
        (0, every(stage_v)),
        (1, []),
        (6, every(stage_glu) + convs[0:1]),
        (5, convs[1:2] + gates[0:2]),
        (3, convs[2:3] + gates[2:4]),
        (4, convs[3:4] + every(stage_gated)),
        (2, convs[4:5] + every(gated_conv)),
        (None, convs[5:6] + every(stage_pool) + convs[6:8] + every(pooling)),
    ]
    for k, pieces in plan:
        if k is not None:
            project(k)
        for piece in pieces:
            piece()

    @pl.when(s == pl.num_programs(1) - 1)
    def _():
        sta_ref[...] = pltpu.roll(gbuf[TS:TS + HALO_A, :], CONV_A_W - 1, axis=0)[0:CONV_A_W - 1, :]
        stp_ref[...] = pltpu.roll(pbuf[TS:TS + HALO_P, :], POOL_BUF, axis=0)[0:POOL_BUF, :]
        stc_ref[...] = pltpu.roll(cbuf[TS:TS + HALO_C, :], SCONV_W - 1, axis=0)[0:SCONV_W - 1, :]


def _mixer_prompt(xs, ss, w, p, layer):
    n_s = SEQ // TS

    def lp(shape):
        nd = len(shape)
        return pl.BlockSpec((None,) + shape, lambda b, s: (layer,) + (0,) * nd)

    state = lambda n: pl.BlockSpec((None, n, D_GROUP), lambda b, s: (b, 0, 0))
    return pl.pallas_call(
        _mixer_prompt_kernel,
        grid=(BATCH, n_s),
        in_specs=[
            pl.BlockSpec((TS, D_MODEL), lambda b, s: (b * n_s + s, 0)),
            pl.BlockSpec((TS, 1), lambda b, s: (b * n_s + s, 0)),
            pl.BlockSpec((None, D_MODEL, D_PROJ), lambda b, s: (layer, 0, 0), pipeline_mode=pl.Buffered(1)),
            lp((CONV_A_W, D_GROUP)), lp((1, D_GROUP)), lp((1, D_GROUP)), lp((1, D_GROUP)),
            lp((N_SUB, D_SUB, D_SUB)), lp((1, D_GROUP)), lp((SCONV_W, D_GROUP)),
            lp((1, D_GROUP)), lp((1, D_GROUP)), lp((N_SUB, CHUNK, CHUNK)), lp((CHUNK, N_SUB)),
            lp((1, D_MODEL)),
        ],
        out_specs=[
            pl.BlockSpec((TS, D_MODEL), lambda b, s: (b * n_s + s, 0)),
            state(CONV_A_W - 1), state(POOL_BUF), state(SCONV_W - 1),
        ],
        out_shape=[
            jax.ShapeDtypeStruct((M_ALL, D_MODEL), BF16),
            jax.ShapeDtypeStruct((BATCH, CONV_A_W - 1, D_GROUP), F32),
            jax.ShapeDtypeStruct((BATCH, POOL_BUF, D_GROUP), F32),
            jax.ShapeDtypeStruct((BATCH, SCONV_W - 1, D_GROUP), F32),
        ],
        scratch_shapes=[
            pltpu.VMEM((HALO_A + TS, D_GROUP), F32),
            pltpu.VMEM((HALO_P + TS, D_GROUP), F32),
            pltpu.VMEM((HALO_C + TS, D_GROUP), F32),
            pltpu.VMEM((TS, D_GROUP), F32),
        ] + [pltpu.VMEM((TS, D_GROUP), F32)] * N_PROJ,
        compiler_params=_params(("arbitrary", "arbitrary")),
        name=f"mixer_prompt_l{layer}",
    )(xs, ss, w, p["conv_a_w"], p["conv_a_b"], p["ln_a_g"], p["ln_a_b"], p["pool_w"], p["pool_scale"],
      p["sconv_w"], p["sgu_ln_g"], p["sgu_ln_b"], p["sgu_w"], p["sgu_b_t"], p["out_norm_g"])


N_SAMPLE_IN = 16
N_SAMPLE_OUT = 5


def _mixer_sample_kernel(*refs, layer):
    (wsm_ref, bsm_ref, proj_ref, sa_ref, sp_ref, sc_ref, caw_ref, cab_ref, lag_ref, lab_ref, pw_ref,
     ps_ref, scw_ref, sg_ref, sb_ref, ong_ref) = refs[:N_SAMPLE_IN]
    mix_ref, nsa_ref, nsp_ref, nsc_ref, vn_ref = refs[-N_SAMPLE_OUT:]

    def col(t, k):
        return proj_ref[t, :, k * D_GROUP:(k + 1) * D_GROUP]

    def slab(j):
        return slice(j * D_GROUP, (j + 1) * D_GROUP)

    steps = range(DEC_SEQ)

    glu = [col(t, 0) * jax.nn.sigmoid(col(t, 1)) for t in steps]
    n_a = CONV_A_W - 1

    def za(j):
        return sa_ref[:, slab(j)] if j < n_a else glu[j - n_a]

    cab = cab_ref[...]
    lag = lag_ref[...]
    lab = lab_ref[...]
    for t in steps:
        acc = caw_ref[0:1, :] * za(t)
        for k in range(1, CONV_A_W):
            acc = acc + caw_ref[k:k + 1, :] * za(t + k)
        ya = _layernorm(acc + cab, lag, lab)
        ya = ya * jax.nn.sigmoid(ya)
        mix_ref[t, :, _mixer(0)] = _rms(ya, ong_ref[:, _mixer(0)]).astype(BF16)
    for j in range(n_a):
        nsa_ref[:, slab(j)] = za(j + DEC_SEQ)

    pin = [col(t, 2) for t in steps]

    def zp(j):
        return sp_ref[:, slab(j)] if j < POOL_BUF else pin[j - POOL_BUF]

    ps = ps_ref[...]
    for t in steps:
        yb = []
        for g, w in enumerate(POOL_WINDOWS):
            cnt = float(min(PAST_LEN + t + 1, w))
            tot = zp(POOL_BUF + t)[:, _group(g)]
            for d in range(1, w):
                tot = tot + zp(POOL_BUF + t - d)[:, _group(g)]
            pooled = tot / cnt - pin[t][:, _group(g)]
            yb.append(jnp.dot(pooled.astype(BF16), pw_ref[g].astype(BF16), preferred_element_type=F32))
        yb = jnp.concatenate(yb, axis=-1) * ps
        mix_ref[t, :, _mixer(1)] = _rms(yb, ong_ref[:, _mixer(1)]).astype(BF16)
    for j in range(POOL_BUF):
        nsp_ref[:, slab(j)] = zp(j + DEC_SEQ)

    gated = [col(t, 5) * col(t, 3) for t in steps]
    n_c = SCONV_W - 1

    def zc(j):
        return sc_ref[:, slab(j)] if j < n_c else gated[j - n_c]

    for t in steps:
        conv = scw_ref[0:1, :] * zc(t) + scw_ref[1:2, :] * zc(t + 1) + scw_ref[2:3, :] * zc(t + 2)
        yc = col(t, 4) * conv
        mix_ref[t, :, _mixer(2)] = _rms(yc, ong_ref[:, _mixer(2)]).astype(BF16)
    for j in range(n_c):
        nsc_ref[:, slab(j)] = zc(j + DEC_SEQ)

    sg = sg_ref[...]
    sb = sb_ref[...]
    vn = [_layernorm(col(t, 7), sg, sb) for t in steps]
    for t in steps:
        vn_ref[:, slab(t)] = vn[t]
    for t in steps:
        z = []
        for g in range(N_SUB):
            base = (layer * N_SUB + g) * DEC_SEQ * DEC_SEQ + t * DEC_SEQ
            zg = wsm_ref[base] * vn[0][:, _group(g)]
            for j in range(1, t + 1):
                zg = zg + wsm_ref[base + j] * vn[j][:, _group(g)]
            z.append(zg + bsm_ref[(layer * N_SUB + g) * DEC_SEQ + t])
        yd = col(t, 6) * jnp.concatenate(z, axis=-1)
        mix_ref[t, :, _mixer(3)] = _rms(yd, ong_ref[:, _mixer(3)]).astype(BF16)


def _mixer_sample(proj, mix, states, stacked, p, layer):
    sa, sp, sc = states
    slabs = M_ALL // DEC_BATCH
    first = M_P // DEC_BATCH // DEC_SEQ
    proj3 = proj.reshape(DEC_SEQ, DEC_BATCH, D_PROJ)
    mix3 = mix.reshape(slabs, DEC_BATCH, D_MODEL)
    in_place = (mix3,) + tuple(stacked)

    def lp(shape):
        nd = len(shape)
        return pl.BlockSpec((None,) + shape, lambda i: (layer,) + (0,) * nd)

    smem = pl.BlockSpec(memory_space=pltpu.SMEM)
    st = lambda n: pl.BlockSpec((None, BB, n * D_GROUP), lambda i: (layer, i, 0))
    st_shape = lambda n: jax.ShapeDtypeStruct((DEPTH, DEC_BATCH, n * D_GROUP), F32)
    n_a, n_c = CONV_A_W - 1, SCONV_W - 1
    outs = pl.pallas_call(
        functools.partial(_mixer_sample_kernel, layer=layer),
        grid=(DEC_BATCH // BB,),
        in_specs=[
            smem, smem,
            pl.BlockSpec((DEC_SEQ, BB, D_PROJ), lambda i: (0, i, 0)),
            st(n_a), st(POOL_BUF), st(n_c),
            lp((CONV_A_W, D_GROUP)), lp((1, D_GROUP)), lp((1, D_GROUP)), lp((1, D_GROUP)),
            lp((N_SUB, D_SUB, D_SUB)), lp((1, D_GROUP)), lp((SCONV_W, D_GROUP)),
            lp((1, D_GROUP)), lp((1, D_GROUP)), lp((1, D_MODEL)),
        ] + [pl.BlockSpec(memory_space=pl.ANY)] * len(in_place),
        out_specs=[
            pl.BlockSpec((DEC_SEQ, BB, D_MODEL), lambda i: (first, i, 0)),
            st(n_a), st(POOL_BUF), st(n_c), st(DEC_SEQ),
        ],
        out_shape=[
            jax.ShapeDtypeStruct((slabs, DEC_BATCH, D_MODEL), BF16),
            st_shape(n_a), st_shape(POOL_BUF), st_shape(n_c), st_shape(DEC_SEQ),
        ],
        input_output_aliases={N_SAMPLE_IN + k: k for k in range(len(in_place))},
        compiler_params=_params(("arbitrary",)),
        name=f"mixer_sample_l{layer}",
    )(p["sgu_w_small"], p["sgu_b_small"], proj3, sa, sp, sc,
      p["conv_a_w"], p["conv_a_b"], p["ln_a_g"], p["ln_a_b"], p["pool_w"], p["pool_scale"],
      p["sconv_w"], p["sgu_ln_g"], p["sgu_ln_b"], p["out_norm_g"], *in_place)
    return outs[0].reshape(M_ALL, D_MODEL), tuple(outs[1:])


FINAL_ROWS = 512


def _final_norm_kernel(x_ref, ss_ref, g_ref, o_ref):
    o_ref[...] = x_ref[...] * _row_scale(ss_ref) * g_ref[...]


def _final_norm(x, ss, g):
    def call(n_rows, first_block, name):
        rows = lambda c: pl.BlockSpec((FINAL_ROWS, c), lambda i: (first_block + i, 0))
        return pl.pallas_call(
            _final_norm_kernel,
            grid=(n_rows // FINAL_ROWS,),
            in_specs=[rows(D_MODEL), rows(1), pl.BlockSpec((1, D_MODEL), lambda i: (0, 0))],
            out_specs=pl.BlockSpec((FINAL_ROWS, D_MODEL), lambda i: (i, 0)),
            out_shape=jax.ShapeDtypeStruct((n_rows, D_MODEL), F32),
            compiler_params=_params(("arbitrary",)),
            name=name,
        )(x, ss, g)

    y_p = call(M_P, 0, "final_norm_prompt")
    y_s = call(M_S, M_P // FINAL_ROWS, "final_norm_sample")
    return (y_p.reshape(BATCH, SEQ, D_MODEL),
            jnp.swapaxes(y_s.reshape(DEC_SEQ, DEC_BATCH, D_MODEL), 0, 1))


def kernel(x_prompt, x_sample, state_conv_a, state_pool, state_sconv, norm_mix_g, w_in, conv_a_w, conv_a_b, ln_a_g, ln_a_b, pool_w, pool_scale, sconv_w, sgu_ln_g, sgu_ln_b, sgu_w, sgu_b, out_norm_g, w_out, norm_ffn_g, w_gate, w_up, w_down, final_norm_g):
    row = lambda a: a.reshape(DEPTH, 1, a.shape[-1])
    p = {
        "conv_a_w": conv_a_w, "conv_a_b": row(conv_a_b), "ln_a_g": row(ln_a_g), "ln_a_b": row(ln_a_b),
        "pool_w": pool_w, "pool_scale": row(pool_scale), "sconv_w": sconv_w,
        "sgu_ln_g": row(sgu_ln_g), "sgu_ln_b": row(sgu_ln_b), "sgu_w": sgu_w,
        "sgu_b_t": jnp.swapaxes(sgu_b, 1, 2),
        "sgu_w_small": sgu_w[:, :, :DEC_SEQ, :DEC_SEQ].reshape(-1),
        "sgu_b_small": sgu_b[:, :, :DEC_SEQ].reshape(-1),
        "out_norm_g": row(out_norm_g),
    }
    g_mix = row(norm_mix_g)
    g_ffn = row(norm_ffn_g)
    w_in_b = w_in.astype(BF16)
    w_down_b = w_down.astype(BF16)

    states = (state_conv_a.reshape(DEPTH, DEC_BATCH, -1), state_pool.reshape(DEPTH, DEC_BATCH, -1),
              state_sconv.reshape(DEPTH, DEC_BATCH, -1))

    x, xs, ss = _prep(x_prompt.reshape(M_P, D_MODEL),
                      jnp.swapaxes(x_sample, 0, 1).reshape(M_S, D_MODEL), g_mix)

    ca_p, pl_p, sc_p = [], [], []
    stacked = ()
    for l in range(DEPTH):
        last = l == DEPTH - 1
        mix, a_p, p_p, c_p = _mixer_prompt(xs, ss, w_in_b, p, l)
        proj_s = _in_proj_sample(xs, ss, w_in_b, l)
        mix, stacked = _mixer_sample(proj_s, mix, states, stacked, p, l)
        ca_p.append(a_p); pl_p.append(p_p); sc_p.append(c_p)
        x, xs, ss = _out_proj(mix, w_out, x, g_ffn, l)
        h = _ffn_up(xs, ss, w_gate, w_up, l)
        x, xs, ss = _ffn_down(h, w_down_b, x, g_mix, l, not last)

    y_prompt, y_sample = _final_norm(x, ss, final_norm_g.reshape(1, D_MODEL))
    ca_s, pl_s, sc_s, v_s = stacked
    return (y_prompt, y_sample, jnp.stack(ca_p),
            ca_s.reshape(DEPTH, DEC_BATCH, CONV_A_W - 1, D_GROUP), jnp.stack(pl_p),
            pl_s.reshape(DEPTH, DEC_BATCH, POOL_BUF, D_GROUP), jnp.stack(sc_p),
            sc_s.reshape(DEPTH, DEC_BATCH, SCONV_W - 1, D_GROUP),
            v_s.reshape(DEPTH, DEC_BATCH, DEC_SEQ, D_GROUP))
```

```python
import functools

import jax
import jax.numpy as jnp
from jax import lax
from jax.experimental import pallas as pl
from jax.experimental.pallas import tpu as pltpu

D_MODEL = 2048
BATCH = 4
SEQ = 2048
DEPTH = 4
DEC_BATCH = 128
DEC_SEQ = 4
PAST_LEN = 16384
D_GROUP = 512
N_SUB = 4
D_SUB = 128
N_PROJ = 8
D_PROJ = N_PROJ * D_GROUP
CONV_A_W = 31
POOL_WINDOWS = (2, 4, 8, 16)
POOL_BUF = 15
SCONV_W = 3
CHUNK = 128
D_FF = 5632
EPS = 1e-6

M_P = BATCH * SEQ
M_S = DEC_BATCH * DEC_SEQ
M_ALL = M_P + M_S

F32 = jnp.float32
BF16 = jnp.bfloat16

V7X_VMEM_LIMIT_BYTES = 56 * 1024 * 1024
V7X_VMEM_LIMIT_BIG_BYTES = 60 * 1024 * 1024

BM = 1088
BM_OUT = 544
BN_IN = 1024
BN_FF = 512
BN_DOWN = 512
CAST_ROWS = 256
PREP_ROWS = 512

BM_FF = 2176

TS = 512
RC = 64
HALO_A = 32
HALO_P = 16
HALO_C = 8

BB = 32


def _params(sem, vmem=V7X_VMEM_LIMIT_BYTES):
    return pltpu.CompilerParams(dimension_semantics=sem, vmem_limit_bytes=vmem)


def _cast_rows_to_bf16(w_ref, wb_ref):
    def body(i, carry):
        r = pl.multiple_of(i * CAST_ROWS, CAST_ROWS)
        wb_ref[pl.ds(r, CAST_ROWS), :] = w_ref[pl.ds(r, CAST_ROWS), :].astype(BF16)
        return carry

    lax.fori_loop(0, w_ref.shape[0] // CAST_ROWS, body, 0)


def _row_scale(ss_ref):
    return lax.rsqrt(ss_ref[...] * (1.0 / D_MODEL) + EPS)


def _prep_kernel(xp_ref, xt_ref, g_ref, x_ref, xs_ref, ss_ref):
    def emit(x):
        x_ref[...] = x
        xs_ref[...] = (x * g_ref[...]).astype(BF16)
        ss_ref[...] = jnp.sum(x * x, axis=-1, keepdims=True)

    i = pl.program_id(0)

    @pl.when(i < M_P // PREP_ROWS)
    def _():
        emit(xp_ref[...])

    @pl.when(i >= M_P // PREP_ROWS)
    def _():
        emit(xt_ref[...])


def _prep(x_prompt, x_sample_tm, g):
    n_p = M_P // PREP_ROWS
    rows = pl.BlockSpec((PREP_ROWS, D_MODEL), lambda i: (i, 0))
    return pl.pallas_call(
        _prep_kernel,
        grid=(M_ALL // PREP_ROWS,),
        in_specs=[
            pl.BlockSpec((PREP_ROWS, D_MODEL), lambda i: (jnp.minimum(i, n_p - 1), 0)),
            pl.BlockSpec((PREP_ROWS, D_MODEL), lambda i: (jnp.maximum(i - n_p, 0), 0)),
            pl.BlockSpec((None, 1, D_MODEL), lambda i: (0, 0, 0)),
        ],
        out_specs=[rows, rows, pl.BlockSpec((PREP_ROWS, 1), lambda i: (i, 0))],
        out_shape=[
            jax.ShapeDtypeStruct((M_ALL, D_MODEL), F32),
            jax.ShapeDtypeStruct((M_ALL, D_MODEL), BF16),
            jax.ShapeDtypeStruct((M_ALL, 1), F32),
        ],
        compiler_params=_params(("arbitrary",)),
        name="prep",
    )(x_prompt, x_sample_tm, g)


def _in_proj_sample_kernel(xs_ref, ss_ref, w_ref, o_ref):
    o_ref[...] = jnp.dot(xs_ref[...], w_ref[...], preferred_element_type=F32) * _row_scale(ss_ref)


def _in_proj_sample(xs, ss, w, layer):
    first = M_P // M_S
    return pl.pallas_call(
        _in_proj_sample_kernel,
        grid=(D_PROJ // BN_IN,),
        in_specs=[
            pl.BlockSpec((M_S, D_MODEL), lambda j: (first, 0)),
            pl.BlockSpec((M_S, 1), lambda j: (first, 0)),
            pl.BlockSpec((D_MODEL, BN_IN), lambda j: (0, j)),
        ],
        out_specs=pl.BlockSpec((M_S, BN_IN), lambda j: (0, j)),
        out_shape=jax.ShapeDtypeStruct((M_S, D_PROJ), F32),
        compiler_params=_params(("arbitrary",)),
        name=f"in_proj_sample_l{layer}",
    )(xs, ss, w)


def _ffn_up_kernel(xs_ref, ss_ref, wg_ref, wu_ref, o_ref, wgb_ref, wub_ref):
    @pl.when(pl.program_id(1) == 0)
    def _():
        _cast_rows_to_bf16(wg_ref, wgb_ref)
        _cast_rows_to_bf16(wu_ref, wub_ref)

    xs = xs_ref[...]
    scale = _row_scale(ss_ref)
    gate = jnp.dot(xs, wgb_ref[...], preferred_element_type=F32) * scale
    up = jnp.dot(xs, wub_ref[...], preferred_element_type=F32) * scale
    o_ref[...] = (jax.nn.silu(gate) * up).astype(BF16)


def _ffn_up(xs, ss, wg, wu, layer):
    return pl.pallas_call(
        _ffn_up_kernel,
        grid=(D_FF // BN_FF, M_ALL // BM_FF),
        in_specs=[
            pl.BlockSpec((BM_FF, D_MODEL), lambda j, i: (i, 0)),
            pl.BlockSpec((BM_FF, 1), lambda j, i: (i, 0)),
            pl.BlockSpec((None, D_MODEL, BN_FF), lambda j, i: (layer, 0, j)),
            pl.BlockSpec((None, D_MODEL, BN_FF), lambda j, i: (layer, 0, j)),
        ],
        out_specs=pl.BlockSpec((BM_FF, BN_FF), lambda j, i: (i, j)),
        out_shape=jax.ShapeDtypeStruct((M_ALL, D_FF), BF16),
        scratch_shapes=[pltpu.VMEM((D_MODEL, BN_FF), BF16), pltpu.VMEM((D_MODEL, BN_FF), BF16)],
        compiler_params=_params(("arbitrary", "arbitrary"), V7X_VMEM_LIMIT_BIG_BYTES),
        name=f"ffn_up_l{layer}",
    )(xs, ss, wg, wu)


def _out_proj_kernel(a_ref, w_ref, r_ref, g_ref, x_ref, xs_ref, ss_ref, wb_ref):
    @pl.when(pl.program_id(0) == 0)
    def _():
        _cast_rows_to_bf16(w_ref, wb_ref)

    o = r_ref[...] + jnp.dot(a_ref[...], wb_ref[...], preferred_element_type=F32)
    x_ref[...] = o
    xs_ref[...] = (o * g_ref[...]).astype(BF16)
    ss_ref[...] = jnp.sum(o * o, axis=-1, keepdims=True)


def _out_proj(mix, w, res, g_next, layer):
    rows = pl.BlockSpec((BM_OUT, D_MODEL), lambda i: (i, 0))
    return pl.pallas_call(
        _out_proj_kernel,
        grid=(M_ALL // BM_OUT,),
        in_specs=[
            rows,
            pl.BlockSpec((None, D_MODEL, D_MODEL), lambda i: (layer, 0, 0), pipeline_mode=pl.Buffered(1)),
            rows,
            pl.BlockSpec((None, 1, D_MODEL), lambda i: (layer, 0, 0)),
        ],
        out_specs=[rows, rows, pl.BlockSpec((BM_OUT, 1), lambda i: (i, 0))],
        out_shape=[
            jax.ShapeDtypeStruct((M_ALL, D_MODEL), F32),
            jax.ShapeDtypeStruct((M_ALL, D_MODEL), BF16),
            jax.ShapeDtypeStruct((M_ALL, 1), F32),
        ],
        scratch_shapes=[pltpu.VMEM((D_MODEL, D_MODEL), BF16)],
        compiler_params=_params(("arbitrary",)),
        name=f"out_proj_l{layer}",
    )(mix, w, res, g_next)


def _ffn_down_kernel(*refs, has_next):
    if has_next:
        h_ref, w_ref, r_ref, g_ref, win_ref, x_ref, xs_ref, ss_ref, winb_ref = refs
        winb_ref[...] = win_ref[...].astype(BF16)
    else:
        h_ref, w_ref, r_ref, x_ref, ss_ref = refs
    o = r_ref[...] + jnp.dot(h_ref[...], w_ref[...], preferred_element_type=F32)
    x_ref[...] = o
    if has_next:
        xs_ref[...] = (o * g_ref[...]).astype(BF16)
    part = jnp.sum(o * o, axis=-1, keepdims=True)
    j = pl.program_id(1)

    @pl.when(j == 0)
    def _():
        ss_ref[...] = part

    @pl.when(j > 0)
    def _():
        ss_ref[...] += part


def _ffn_down(h, w, res, g_next, w_in, layer):
    has_next = layer + 1 < DEPTH
    n_j = D_MODEL // BN_DOWN
    cast_rows = D_MODEL // (M_ALL // BM * n_j)
    tile = pl.BlockSpec((BM, BN_DOWN), lambda i, j: (i, j))
    ss_spec = pl.BlockSpec((BM, 1), lambda i, j: (i, 0))
    x_shape = jax.ShapeDtypeStruct((M_ALL, D_MODEL), F32)
    ss_shape = jax.ShapeDtypeStruct((M_ALL, 1), F32)
    in_specs = [pl.BlockSpec((BM, D_FF), lambda i, j: (i, 0)),
                pl.BlockSpec((D_FF, BN_DOWN), lambda i, j: (0, j)),
                tile]
    args = [h, w, res]
    out_specs, out_shape = [tile, ss_spec], [x_shape, ss_shape]
    if has_next:
        in_specs += [pl.BlockSpec((None, 1, BN_DOWN), lambda i, j: (layer + 1, 0, j)),
                     pl.BlockSpec((None, cast_rows, D_PROJ), lambda i, j: (layer + 1, i * n_j + j, 0))]
        args += [g_next, w_in]
        out_specs = [tile, tile, ss_spec, pl.BlockSpec((cast_rows, D_PROJ), lambda i, j: (i * n_j + j, 0))]
        out_shape = [x_shape, jax.ShapeDtypeStruct((M_ALL, D_MODEL), BF16), ss_shape,
                     jax.ShapeDtypeStruct((D_MODEL, D_PROJ), BF16)]
    outs = pl.pallas_call(
        functools.partial(_ffn_down_kernel, has_next=has_next),
        grid=(M_ALL // BM, n_j),
        in_specs=in_specs,
        out_specs=out_specs,
        out_shape=out_shape,
        compiler_params=_params(("arbitrary", "arbitrary")),
        name=f"ffn_down_l{layer}",
    )(*args)
    return tuple(outs) if has_next else (outs[0], None, outs[1], None)


def _layernorm(x, g, b):
    mu = jnp.mean(x, axis=-1, keepdims=True)
    xc = x - mu
    var = jnp.mean(xc * xc, axis=-1, keepdims=True)
    return xc * lax.rsqrt(var + EPS) * g + b


def _rms(x, g):
    return x * lax.rsqrt(jnp.mean(x * x, axis=-1, keepdims=True) + EPS) * g


def _group(g):
    return slice(g * D_SUB, (g + 1) * D_SUB)


def _mixer(m):
    return slice(m * D_GROUP, (m + 1) * D_GROUP)


def _mixer_prompt_kernel(xs_ref, ss_ref, w_ref, caw_ref, cab_ref, lag_ref, lab_ref, pw_ref, ps_ref, scw_ref,
                         sg_ref, sb_ref, sw_ref, sbias_ref, ong_ref, wd_ref,
                         mix_ref, sta_ref, stp_ref, stc_ref, wdb_ref,
                         gbuf, pbuf, cbuf, vbuf, *proj_refs):
    s = pl.program_id(1)
    streams = ((gbuf, HALO_A), (pbuf, HALO_P), (cbuf, HALO_C))

    @pl.when(s == 0)
    def _():
        for buf, halo in streams:
            buf[:, 0:halo, :] = jnp.zeros((N_SUB, halo, D_SUB), F32)

    @pl.when(s > 0)
    def _():
        for buf, halo in streams:
            buf[:, 0:halo, :] = buf[:, TS:TS + halo, :]

    scale = _row_scale(ss_ref)
    sg = sg_ref[...]
    sb = sb_ref[...]
    cab = cab_ref[...]
    lag = lag_ref[...]
    lab = lab_ref[...]
    ps = ps_ref[...]
    rows = lambda c: slice(c * RC, (c + 1) * RC)

    def project(k):
        proj_refs[k][...] = jnp.dot(xs_ref[...], w_ref[:, k * D_GROUP:(k + 1) * D_GROUP],
                                    preferred_element_type=F32) * scale

    def col(k, c, n=RC):
        return proj_refs[k][c * n:(c + 1) * n, :]

    def put(buf, halo, c, val):
        for j in range(N_SUB):
            buf[j, halo + c * RC:halo + (c + 1) * RC, :] = val[:, _group(j)]

    def window(buf, halo, j, c, back):
        lo = halo + c * RC - back
        return buf[j, lo:lo + RC, :]

    def stage_glu(c):
        put(gbuf, HALO_A, c, col(0, c) * jax.nn.sigmoid(col(1, c)))

    def stage_pool(c):
        put(pbuf, HALO_P, c, col(2, c))

    def stage_gated(c):
        put(cbuf, HALO_C, c, col(5, c) * col(3, c))

    def stage_v(c):
        vbuf[rows(c), :] = _layernorm(col(7, c), sg, sb)

    def conv_module(c):
        tiles = []
        for j in range(N_SUB):
            acc = None
            for k in range(CONV_A_W):
                term = caw_ref[k:k + 1, _group(j)] * window(gbuf, HALO_A, j, c, CONV_A_W - 1 - k)
                acc = term if acc is None else acc + term
            tiles.append(acc)
        ya = _layernorm(jnp.concatenate(tiles, axis=-1) + cab, lag, lab)
        ya = ya * jax.nn.sigmoid(ya)
        mix_ref[rows(c), _mixer(0)] = _rms(ya, ong_ref[:, _mixer(0)]).astype(BF16)

    def pooling(c):
        pos = s * TS + c * RC + lax.broadcasted_iota(jnp.int32, (RC, D_SUB), 0)
        yb = []
        for g, w in enumerate(POOL_WINDOWS):
            cur = window(pbuf, HALO_P, g, c, 0)
            tot = cur
            for d in range(1, w):
                tot = tot + window(pbuf, HALO_P, g, c, d)
            cnt = jnp.minimum(pos + 1, w).astype(F32)
            pooled = tot / cnt - cur
            yb.append(jnp.dot(pooled.astype(BF16), pw_ref[g].astype(BF16), preferred_element_type=F32))
        yb = jnp.concatenate(yb, axis=-1) * ps
        mix_ref[rows(c), _mixer(1)] = _rms(yb, ong_ref[:, _mixer(1)]).astype(BF16)

    def gated_conv(c):
        tiles = []
        for j in range(N_SUB):
            acc = None
            for k in range(SCONV_W):
                term = scw_ref[k:k + 1, _group(j)] * window(cbuf, HALO_C, j, c, SCONV_W - 1 - k)
                acc = term if acc is None else acc + term
            tiles.append(acc)
        yc = col(4, c) * jnp.concatenate(tiles, axis=-1)
        mix_ref[rows(c), _mixer(2)] = _rms(yc, ong_ref[:, _mixer(2)]).astype(BF16)

    row = lax.broadcasted_iota(jnp.int32, (CHUNK, CHUNK), 0)
    colid = lax.broadcasted_iota(jnp.int32, (CHUNK, CHUNK), 1)
    tril = (row >= colid).astype(F32)

    def spatial_gate(c):
        r0 = c * CHUNK
        v = vbuf[r0:r0 + CHUNK, :].astype(BF16)
        z = [jnp.dot((sw_ref[g] * tril).astype(BF16), v[:, _group(g)], preferred_element_type=F32)
             + sbias_ref[:, g:g + 1] for g in range(N_SUB)]
        yd = col(6, c, CHUNK) * jnp.concatenate(z, axis=-1)
        mix_ref[r0:r0 + CHUNK, _mixer(3)] = _rms(yd, ong_ref[:, _mixer(3)]).astype(BF16)

    every = lambda f: [functools.partial(f, c) for c in range(TS // RC)]
    convs = every(conv_module)
    gates = [functools.partial(spatial_gate, c) for c in range(TS // CHUNK)]
    plan = [
        (7, []),
        (0, every(stage_v)),
        (1, []),
        (6, every(stage_glu) + convs[0:1]),
        (5, convs[1:2] + gates[0:2]),
        (3, convs[2:3] + gates[2:4]),
        (4, convs[3:4] + every(stage_gated)),
        (2, convs[4:5] + every(gated_conv)),
        (None, convs[5:6] + every(stage_pool) + convs[6:8] + every(pooling)),
    ]
    for k, pieces in plan:
        if k is not None:
            project(k)
        for piece in pieces:
            piece()

    wdb_ref[...] = wd_ref[...].astype(BF16)

    @pl.when(s == pl.num_programs(1) - 1)
    def _():
        def tail(buf, halo, n):
            return jnp.concatenate([buf[j, halo + TS - n:halo + TS, :] for j in range(N_SUB)], axis=-1)

        sta_ref[...] = tail(gbuf, HALO_A, CONV_A_W - 1)
        stp_ref[...] = tail(pbuf, HALO_P, POOL_BUF)
        stc_ref[...] = tail(cbuf, HALO_C, SCONV_W - 1)


def _mixer_prompt(xs, ss, w, w_down, p, layer):
    n_s = SEQ // TS
    wd_rows = D_FF // (BATCH * n_s)

    def lp(shape):
        nd = len(shape)
        return pl.BlockSpec((None,) + shape, lambda b, s: (layer,) + (0,) * nd)

    state = lambda n: pl.BlockSpec((None, n, D_GROUP), lambda b, s: (b, 0, 0))
    return pl.pallas_call(
        _mixer_prompt_kernel,
        grid=(BATCH, n_s),
        in_specs=[
            pl.BlockSpec((TS, D_MODEL), lambda b, s: (b * n_s + s, 0)),
            pl.BlockSpec((TS, 1), lambda b, s: (b * n_s + s, 0)),
            pl.BlockSpec((D_MODEL, D_PROJ), lambda b, s: (0, 0), pipeline_mode=pl.Buffered(1)),
            lp((CONV_A_W, D_GROUP)), lp((1, D_GROUP)), lp((1, D_GROUP)), lp((1, D_GROUP)),
            lp((N_SUB, D_SUB, D_SUB)), lp((1, D_GROUP)), lp((SCONV_W, D_GROUP)),
            lp((1, D_GROUP)), lp((1, D_GROUP)), lp((N_SUB, CHUNK, CHUNK)), lp((CHUNK, N_SUB)),
            lp((1, D_MODEL)),
            pl.BlockSpec((None, wd_rows, D_MODEL), lambda b, s: (layer, b * n_s + s, 0)),
        ],
        out_specs=[
            pl.BlockSpec((TS, D_MODEL), lambda b, s: (b * n_s + s, 0)),
            state(CONV_A_W - 1), state(POOL_BUF), state(SCONV_W - 1),
            pl.BlockSpec((wd_rows, D_MODEL), lambda b, s: (b * n_s + s, 0)),
        ],
        out_shape=[
            jax.ShapeDtypeStruct((M_ALL, D_MODEL), BF16),
            jax.ShapeDtypeStruct((BATCH, CONV_A_W - 1, D_GROUP), F32),
            jax.ShapeDtypeStruct((BATCH, POOL_BUF, D_GROUP), F32),
            jax.ShapeDtypeStruct((BATCH, SCONV_W - 1, D_GROUP), F32),
            jax.ShapeDtypeStruct((D_FF, D_MODEL), BF16),
        ],
        scratch_shapes=[
            pltpu.VMEM((N_SUB, HALO_A + TS, D_SUB), F32),
            pltpu.VMEM((N_SUB, HALO_P + TS, D_SUB), F32),
            pltpu.VMEM((N_SUB, HALO_C + TS, D_SUB), F32),
            pltpu.VMEM((TS, D_GROUP), F32),
        ] + [pltpu.VMEM((TS, D_GROUP), F32)] * N_PROJ,
        compiler_params=_params(("arbitrary", "arbitrary")),
        name=f"mixer_prompt_l{layer}",
    )(xs, ss, w, p["conv_a_w"], p["conv_a_b"], p["ln_a_g"], p["ln_a_b"], p["pool_w"], p["pool_scale"],
      p["sconv_w"], p["sgu_ln_g"], p["sgu_ln_b"], p["sgu_w"], p["sgu_b_t"], p["out_norm_g"], w_down)


N_SAMPLE_IN = 16
N_SAMPLE_OUT = 5


def _mixer_sample_kernel(*refs, layer):
    (wsm_ref, bsm_ref, proj_ref, sa_ref, sp_ref, sc_ref, caw_ref, cab_ref, lag_ref, lab_ref, pw_ref,
     ps_ref, scw_ref, sg_ref, sb_ref, ong_ref) = refs[:N_SAMPLE_IN]
    mix_ref, nsa_ref, nsp_ref, nsc_ref, vn_ref = refs[-N_SAMPLE_OUT:]

    def col(t, k):
        return proj_ref[t, :, k * D_GROUP:(k + 1) * D_GROUP]

    def slab(j):
        return slice(j * D_GROUP, (j + 1) * D_GROUP)

    steps = range(DEC_SEQ)

    glu = [col(t, 0) * jax.nn.sigmoid(col(t, 1)) for t in steps]
    n_a = CONV_A_W - 1

    def za(j):
        return sa_ref[:, slab(j)] if j < n_a else glu[j - n_a]

    cab = cab_ref[...]
    lag = lag_ref[...]
    lab = lab_ref[...]
    for t in steps:
        acc = caw_ref[0:1, :] * za(t)
        for k in range(1, CONV_A_W):
            acc = acc + caw_ref[k:k + 1, :] * za(t + k)
        ya = _layernorm(acc + cab, lag, lab)
        ya = ya * jax.nn.sigmoid(ya)
        mix_ref[t, :, _mixer(0)] = _rms(ya, ong_ref[:, _mixer(0)]).astype(BF16)
    for j in range(n_a):
        nsa_ref[:, slab(j)] = za(j + DEC_SEQ)

    pin = [col(t, 2) for t in steps]

    def zp(j):
        return sp_ref[:, slab(j)] if j < POOL_BUF else pin[j - POOL_BUF]

    ps = ps_ref[...]
    for t in steps:
        yb = []
        for g, w in enumerate(POOL_WINDOWS):
            cnt = float(min(PAST_LEN + t + 1, w))
            tot = zp(POOL_BUF + t)[:, _group(g)]
            for d in range(1, w):
                tot = tot + zp(POOL_BUF + t - d)[:, _group(g)]
            pooled = tot / cnt - pin[t][:, _group(g)]
            yb.append(jnp.dot(pooled.astype(BF16), pw_ref[g].astype(BF16), preferred_element_type=F32))
        yb = jnp.concatenate(yb, axis=-1) * ps
        mix_ref[t, :, _mixer(1)] = _rms(yb, ong_ref[:, _mixer(1)]).astype(BF16)
    for j in range(POOL_BUF):
        nsp_ref[:, slab(j)] = zp(j + DEC_SEQ)

    gated = [col(t, 5) * col(t, 3) for t in steps]
    n_c = SCONV_W - 1

    def zc(j):
        return sc_ref[:, slab(j)] if j < n_c else gated[j - n_c]

    for t in steps:
        conv = scw_ref[0:1, :] * zc(t) + scw_ref[1:2, :] * zc(t + 1) + scw_ref[2:3, :] * zc(t + 2)
        yc = col(t, 4) * conv
        mix_ref[t, :, _mixer(2)] = _rms(yc, ong_ref[:, _mixer(2)]).astype(BF16)
    for j in range(n_c):
        nsc_ref[:, slab(j)] = zc(j + DEC_SEQ)

    sg = sg_ref[...]
    sb = sb_ref[...]
    vn = [_layernorm(col(t, 7), sg, sb) for t in steps]
    for t in steps:
        vn_ref[:, slab(t)] = vn[t]
    for t in steps:
        z = []
        for g in range(N_SUB):
            base = (layer * N_SUB + g) * DEC_SEQ * DEC_SEQ + t * DEC_SEQ
            zg = wsm_ref[base] * vn[0][:, _group(g)]
            for j in range(1, t + 1):
                zg = zg + wsm_ref[base + j] * vn[j][:, _group(g)]
            z.append(zg + bsm_ref[(layer * N_SUB + g) * DEC_SEQ + t])
        yd = col(t, 6) * jnp.concatenate(z, axis=-1)
        mix_ref[t, :, _mixer(3)] = _rms(yd, ong_ref[:, _mixer(3)]).astype(BF16)


def _mixer_sample(proj, mix, states, stacked, p, layer):
    sa, sp, sc = states
    slabs = M_ALL // DEC_BATCH
    first = M_P // DEC_BATCH // DEC_SEQ
    proj3 = proj.reshape(DEC_SEQ, DEC_BATCH, D_PROJ)
    mix3 = mix.reshape(slabs, DEC_BATCH, D_MODEL)
    in_place = (mix3,) + tuple(stacked)

    def lp(shape):
        nd = len(shape)
        return pl.BlockSpec((None,) + shape, lambda i: (layer,) + (0,) * nd)

    smem = pl.BlockSpec(memory_space=pltpu.SMEM)
    st = lambda n: pl.BlockSpec((None, BB, n * D_GROUP), lambda i: (layer, i, 0))
    st_shape = lambda n: jax.ShapeDtypeStruct((DEPTH, DEC_BATCH, n * D_GROUP), F32)
    n_a, n_c = CONV_A_W - 1, SCONV_W - 1
    outs = pl.pallas_call(
        functools.partial(_mixer_sample_kernel, layer=layer),
        grid=(DEC_BATCH // BB,),
        in_specs=[
            smem, smem,
            pl.BlockSpec((DEC_SEQ, BB, D_PROJ), lambda i: (0, i, 0)),
            st(n_a), st(POOL_BUF), st(n_c),
            lp((CONV_A_W, D_GROUP)), lp((1, D_GROUP)), lp((1, D_GROUP)), lp((1, D_GROUP)),
            lp((N_SUB, D_SUB, D_SUB)), lp((1, D_GROUP)), lp((SCONV_W, D_GROUP)),
            lp((1, D_GROUP)), lp((1, D_GROUP)), lp((1, D_MODEL)),
        ] + [pl.BlockSpec(memory_space=pl.ANY)] * len(in_place),
        out_specs=[
            pl.BlockSpec((DEC_SEQ, BB, D_MODEL), lambda i: (first, i, 0)),
            st(n_a), st(POOL_BUF), st(n_c), st(DEC_SEQ),
        ],
        out_shape=[
            jax.ShapeDtypeStruct((slabs, DEC_BATCH, D_MODEL), BF16),
            st_shape(n_a), st_shape(POOL_BUF), st_shape(n_c), st_shape(DEC_SEQ),
        ],
        input_output_aliases={N_SAMPLE_IN + k: k for k in range(len(in_place))},
        compiler_params=_params(("arbitrary",)),
        name=f"mixer_sample_l{layer}",
    )(p["sgu_w_small"], p["sgu_b_small"], proj3, sa, sp, sc,
      p["conv_a_w"], p["conv_a_b"], p["ln_a_g"], p["ln_a_b"], p["pool_w"], p["pool_scale"],
      p["sconv_w"], p["sgu_ln_g"], p["sgu_ln_b"], p["out_norm_g"], *in_place)
    return outs[0].reshape(M_ALL, D_MODEL), tuple(outs[1:])


FINAL_ROWS = 512


def _final_norm_kernel(x_ref, ss_ref, g_ref, o_ref):
    o_ref[...] = x_ref[...] * _row_scale(ss_ref) * g_ref[...]


def _final_norm(x, ss, g):
    def call(n_rows, first_block, name):
        rows = lambda c: pl.BlockSpec((FINAL_ROWS, c), lambda i: (first_block + i, 0))
        return pl.pallas_call(
            _final_norm_kernel,
            grid=(n_rows // FINAL_ROWS,),
            in_specs=[rows(D_MODEL), rows(1), pl.BlockSpec((1, D_MODEL), lambda i: (0, 0))],
            out_specs=pl.BlockSpec((FINAL_ROWS, D_MODEL), lambda i: (i, 0)),
            out_shape=jax.ShapeDtypeStruct((n_rows, D_MODEL), F32),
            compiler_params=_params(("arbitrary",)),
            name=name,
        )(x, ss, g)

    y_p = call(M_P, 0, "final_norm_prompt")
    y_s = call(M_S, M_P // FINAL_ROWS, "final_norm_sample")
    return (y_p.reshape(BATCH, SEQ, D_MODEL),
            jnp.swapaxes(y_s.reshape(DEC_SEQ, DEC_BATCH, D_MODEL), 0, 1))


def kernel(x_prompt, x_sample, state_conv_a, state_pool, state_sconv, norm_mix_g, w_in, conv_a_w, conv_a_b, ln_a_g, ln_a_b, pool_w, pool_scale, sconv_w, sgu_ln_g, sgu_ln_b, sgu_w, sgu_b, out_norm_g, w_out, norm_ffn_g, w_gate, w_up, w_down, final_norm_g):
    row = lambda a: a.reshape(DEPTH, 1, a.shape[-1])
    p = {
        "conv_a_w": conv_a_w, "conv_a_b": row(conv_a_b), "ln_a_g": row(ln_a_g), "ln_a_b": row(ln_a_b),
        "pool_w": pool_w, "pool_scale": row(pool_scale), "sconv_w": sconv_w,
        "sgu_ln_g": row(sgu_ln_g), "sgu_ln_b": row(sgu_ln_b), "sgu_w": sgu_w,
        "sgu_b_t": jnp.swapaxes(sgu_b, 1, 2),
        "sgu_w_small": sgu_w[:, :, :DEC_SEQ, :DEC_SEQ].reshape(-1),
        "sgu_b_small": sgu_b[:, :, :DEC_SEQ].reshape(-1),
        "out_norm_g": row(out_norm_g),
    }
    g_mix = row(norm_mix_g)
    g_ffn = row(norm_ffn_g)
    w_in_b = w_in[0].astype(BF16)

    states = (state_conv_a.reshape(DEPTH, DEC_BATCH, -1), state_pool.reshape(DEPTH, DEC_BATCH, -1),
              state_sconv.reshape(DEPTH, DEC_BATCH, -1))

    x, xs, ss = _prep(x_prompt.reshape(M_P, D_MODEL),
                      jnp.swapaxes(x_sample, 0, 1).reshape(M_S, D_MODEL), g_mix)

    ca_p, pl_p, sc_p = [], [], []
    stacked = ()
    for l in range(DEPTH):
        mix, a_p, p_p, c_p, w_down_b = _mixer_prompt(xs, ss, w_in_b, w_down, p, l)
        proj_s = _in_proj_sample(xs, ss, w_in_b, l)
        mix, stacked = _mixer_sample(proj_s, mix, states, stacked, p, l)
        ca_p.append(a_p); pl_p.append(p_p); sc_p.append(c_p)
        x, xs, ss = _out_proj(mix, w_out, x, g_ffn, l)
        h = _ffn_up(xs, ss, w_gate, w_up, l)
        x, xs, ss, w_in_b = _ffn_down(h, w_down_b, x, g_mix, w_in, l)

    y_prompt, y_sample = _final_norm(x, ss, final_norm_g.reshape(1, D_MODEL))
    ca_s, pl_s, sc_s, v_s = stacked
    return (y_prompt, y_sample, jnp.stack(ca_p),
            ca_s.reshape(DEPTH, DEC_BATCH, CONV_A_W - 1, D_GROUP), jnp.stack(pl_p),
            pl_s.reshape(DEPTH, DEC_BATCH, POOL_BUF, D_GROUP), jnp.stack(sc_p),
            sc_s.reshape(DEPTH, DEC_BATCH, SCONV_W - 1, D_GROUP),
            v_s.reshape(DEPTH, DEC_BATCH, DEC_SEQ, D_GROUP))
```

```python
import functools

import jax
import jax.numpy as jnp
from jax import lax
from jax.experimental import pallas as pl
from jax.experimental.pallas import tpu as pltpu

D_MODEL = 2048
BATCH = 4
SEQ = 2048
DEPTH = 4
DEC_BATCH = 128
DEC_SEQ = 4
PAST_LEN = 16384
D_GROUP = 512
N_SUB = 4
D_SUB = 128
N_PROJ = 8
D_PROJ = N_PROJ * D_GROUP
CONV_A_W = 31
POOL_WINDOWS = (2, 4, 8, 16)
POOL_BUF = 15
SCONV_W = 3
CHUNK = 128
D_FF = 5632
EPS = 1e-6

M_P = BATCH * SEQ
M_S = DEC_BATCH * DEC_SEQ
M_ALL = M_P + M_S

F32 = jnp.float32
BF16 = jnp.bfloat16

V7X_VMEM_LIMIT_BYTES = 56 * 1024 * 1024
V7X_VMEM_LIMIT_BIG_BYTES = 60 * 1024 * 1024

BM = 1088
BM_OUT = 512
BN_IN = 1024
BN_FF = 512
BN_DOWN = 512
CAST_ROWS = 256
PREP_ROWS = 512

BM_FF = 2176

TS = 512
RC = 64
HALO_A = 32
HALO_P = 16
HALO_C = 8

BB = 32


def _params(sem, vmem=V7X_VMEM_LIMIT_BYTES):
    return pltpu.CompilerParams(dimension_semantics=sem, vmem_limit_bytes=vmem)


def _cast_rows_to_bf16(w_ref, wb_ref):
    def body(i, carry):
        r = pl.multiple_of(i * CAST_ROWS, CAST_ROWS)
        wb_ref[pl.ds(r, CAST_ROWS), :] = w_ref[pl.ds(r, CAST_ROWS), :].astype(BF16)
        return carry

    lax.fori_loop(0, w_ref.shape[0] // CAST_ROWS, body, 0)


def _row_scale(ss_ref):
    return lax.rsqrt(ss_ref[...] * (1.0 / D_MODEL) + EPS)


def _two_source_specs(rows):
    n_p = M_P // rows
    return [pl.BlockSpec((rows, D_MODEL), lambda i: (jnp.minimum(i, n_p - 1), 0)),
            pl.BlockSpec((rows, D_MODEL), lambda i: (jnp.maximum(i - n_p, 0), 0))]


def _two_source_tile(xp_ref, xt_ref, rows):
    return jnp.where(pl.program_id(0) < M_P // rows, xp_ref[...], xt_ref[...])


def _prep_kernel(xp_ref, xt_ref, g_ref, xs_ref, ss_ref):
    x = _two_source_tile(xp_ref, xt_ref, PREP_ROWS)
    xs_ref[...] = (x * g_ref[...]).astype(BF16)
    ss_ref[...] = jnp.sum(x * x, axis=-1, keepdims=True)


def _prep(x_prompt, x_sample_tm, g):
    rows = pl.BlockSpec((PREP_ROWS, D_MODEL), lambda i: (i, 0))
    return pl.pallas_call(
        _prep_kernel,
        grid=(M_ALL // PREP_ROWS,),
        in_specs=_two_source_specs(PREP_ROWS) + [pl.BlockSpec((None, 1, D_MODEL), lambda i: (0, 0, 0))],
        out_specs=[rows, pl.BlockSpec((PREP_ROWS, 1), lambda i: (i, 0))],
        out_shape=[
            jax.ShapeDtypeStruct((M_ALL, D_MODEL), BF16),
            jax.ShapeDtypeStruct((M_ALL, 1), F32),
        ],
        compiler_params=_params(("arbitrary",)),
        name="prep",
    )(x_prompt, x_sample_tm, g)


def _in_proj_sample_kernel(xs_ref, ss_ref, w_ref, o_ref):
    o_ref[...] = jnp.dot(xs_ref[...], w_ref[...], preferred_element_type=F32) * _row_scale(ss_ref)


def _in_proj_sample(xs, ss, w, layer):
    first = M_P // M_S
    return pl.pallas_call(
        _in_proj_sample_kernel,
        grid=(D_PROJ // BN_IN,),
        in_specs=[
            pl.BlockSpec((M_S, D_MODEL), lambda j: (first, 0)),
            pl.BlockSpec((M_S, 1), lambda j: (first, 0)),
            pl.BlockSpec((D_MODEL, BN_IN), lambda j: (0, j)),
        ],
        out_specs=pl.BlockSpec((M_S, BN_IN), lambda j: (0, j)),
        out_shape=jax.ShapeDtypeStruct((M_S, D_PROJ), F32),
        compiler_params=_params(("arbitrary",)),
        name=f"in_proj_sample_l{layer}",
    )(xs, ss, w)


def _ffn_up_kernel(xs_ref, ss_ref, wg_ref, wu_ref, o_ref, wgb_ref, wub_ref):
    @pl.when(pl.program_id(1) == 0)
    def _():
        _cast_rows_to_bf16(wg_ref, wgb_ref)
        _cast_rows_to_bf16(wu_ref, wub_ref)

    xs = xs_ref[...]
    scale = _row_scale(ss_ref)
    gate = jnp.dot(xs, wgb_ref[...], preferred_element_type=F32) * scale
    up = jnp.dot(xs, wub_ref[...], preferred_element_type=F32) * scale
    o_ref[...] = (jax.nn.silu(gate) * up).astype(BF16)


def _ffn_up(xs, ss, wg, wu, layer):
    return pl.pallas_call(
        _ffn_up_kernel,
        grid=(D_FF // BN_FF, M_ALL // BM_FF),
        in_specs=[
            pl.BlockSpec((BM_FF, D_MODEL), lambda j, i: (i, 0)),
            pl.BlockSpec((BM_FF, 1), lambda j, i: (i, 0)),
            pl.BlockSpec((None, D_MODEL, BN_FF), lambda j, i: (layer, 0, j)),
            pl.BlockSpec((None, D_MODEL, BN_FF), lambda j, i: (layer, 0, j)),
        ],
        out_specs=pl.BlockSpec((BM_FF, BN_FF), lambda j, i: (i, j)),
        out_shape=jax.ShapeDtypeStruct((M_ALL, D_FF), BF16),
        scratch_shapes=[pltpu.VMEM((D_MODEL, BN_FF), BF16), pltpu.VMEM((D_MODEL, BN_FF), BF16)],
        compiler_params=_params(("arbitrary", "arbitrary"), V7X_VMEM_LIMIT_BIG_BYTES),
        name=f"ffn_up_l{layer}",
    )(xs, ss, wg, wu)


def _out_proj_kernel(a_ref, w_ref, g_ref, *refs):
    *res_refs, x_ref, xs_ref, ss_ref, wb_ref = refs

    @pl.when(pl.program_id(0) == 0)
    def _():
        _cast_rows_to_bf16(w_ref, wb_ref)

    res = res_refs[0][...] if len(res_refs) == 1 else _two_source_tile(*res_refs, BM_OUT)
    o = res + jnp.dot(a_ref[...], wb_ref[...], preferred_element_type=F32)
    x_ref[...] = o
    xs_ref[...] = (o * g_ref[...]).astype(BF16)
    ss_ref[...] = jnp.sum(o * o, axis=-1, keepdims=True)


def _out_proj(mix, w, res, g_next, layer):
    rows = pl.BlockSpec((BM_OUT, D_MODEL), lambda i: (i, 0))
    res = res if isinstance(res, tuple) else (res,)
    return pl.pallas_call(
        _out_proj_kernel,
        grid=(M_ALL // BM_OUT,),
        in_specs=[
            rows,
            pl.BlockSpec((None, D_MODEL, D_MODEL), lambda i: (layer, 0, 0), pipeline_mode=pl.Buffered(1)),
            pl.BlockSpec((None, 1, D_MODEL), lambda i: (layer, 0, 0)),
        ] + ([rows] if len(res) == 1 else _two_source_specs(BM_OUT)),
        out_specs=[rows, rows, pl.BlockSpec((BM_OUT, 1), lambda i: (i, 0))],
        out_shape=[
            jax.ShapeDtypeStruct((M_ALL, D_MODEL), F32),
            jax.ShapeDtypeStruct((M_ALL, D_MODEL), BF16),
            jax.ShapeDtypeStruct((M_ALL, 1), F32),
        ],
        scratch_shapes=[pltpu.VMEM((D_MODEL, D_MODEL), BF16)],
        compiler_params=_params(("arbitrary",), V7X_VMEM_LIMIT_BIG_BYTES),
        name=f"out_proj_l{layer}",
    )(mix, w, g_next, *res)


def _ffn_down_kernel(*refs, has_next):
    if has_next:
        h_ref, w_ref, r_ref, g_ref, win_ref, x_ref, xs_ref, ss_ref, winb_ref = refs
        winb_ref[...] = win_ref[...].astype(BF16)
    else:
        h_ref, w_ref, r_ref, x_ref, ss_ref = refs
    o = r_ref[...] + jnp.dot(h_ref[...], w_ref[...], preferred_element_type=F32)
    x_ref[...] = o
    if has_next:
        xs_ref[...] = (o * g_ref[...]).astype(BF16)
    part = jnp.sum(o * o, axis=-1, keepdims=True)
    j = pl.program_id(1)

    @pl.when(j == 0)
    def _():
        ss_ref[...] = part

    @pl.when(j > 0)
    def _():
        ss_ref[...] += part


def _ffn_down(h, w, res, g_next, w_in, layer):
    has_next = layer + 1 < DEPTH
    n_j = D_MODEL // BN_DOWN
    cast_rows = D_MODEL // (M_ALL // BM * n_j)
    tile = pl.BlockSpec((BM, BN_DOWN), lambda i, j: (i, j))
    ss_spec = pl.BlockSpec((BM, 1), lambda i, j: (i, 0))
    x_shape = jax.ShapeDtypeStruct((M_ALL, D_MODEL), F32)
    ss_shape = jax.ShapeDtypeStruct((M_ALL, 1), F32)
    in_specs = [pl.BlockSpec((BM, D_FF), lambda i, j: (i, 0)),
                pl.BlockSpec((D_FF, BN_DOWN), lambda i, j: (0, j)),
                tile]
    args = [h, w, res]
    out_specs, out_shape = [tile, ss_spec], [x_shape, ss_shape]
    if has_next:
        in_specs += [pl.BlockSpec((None, 1, BN_DOWN), lambda i, j: (layer + 1, 0, j)),
                     pl.BlockSpec((None, cast_rows, D_PROJ), lambda i, j: (layer + 1, i * n_j + j, 0))]
        args += [g_next, w_in]
        out_specs = [tile, tile, ss_spec, pl.BlockSpec((cast_rows, D_PROJ), lambda i, j: (i * n_j + j, 0))]
        out_shape = [x_shape, jax.ShapeDtypeStruct((M_ALL, D_MODEL), BF16), ss_shape,
                     jax.ShapeDtypeStruct((D_MODEL, D_PROJ), BF16)]
    outs = pl.pallas_call(
        functools.partial(_ffn_down_kernel, has_next=has_next),
        grid=(M_ALL // BM, n_j),
        in_specs=in_specs,
        out_specs=out_specs,
        out_shape=out_shape,
        compiler_params=_params(("arbitrary", "arbitrary")),
        name=f"ffn_down_l{layer}",
    )(*args)
    return tuple(outs) if has_next else (outs[0], None, outs[1], None)


def _layernorm(x, g, b):
    mu = jnp.mean(x, axis=-1, keepdims=True)
    xc = x - mu
    var = jnp.mean(xc * xc, axis=-1, keepdims=True)
    return xc * lax.rsqrt(var + EPS) * g + b


def _rms(x, g):
    return x * lax.rsqrt(jnp.mean(x * x, axis=-1, keepdims=True) + EPS) * g


def _group(g):
    return slice(g * D_SUB, (g + 1) * D_SUB)


def _mixer(m):
    return slice(m * D_GROUP, (m + 1) * D_GROUP)


def _mixer_prompt_kernel(xs_ref, ss_ref, w_ref, caw_ref, cab_ref, lag_ref, lab_ref, pw_ref, ps_ref, scw_ref,
                         sg_ref, sb_ref, sw_ref, sbias_ref, ong_ref, wd_ref,
                         mix_ref, sta_ref, stp_ref, stc_ref, wdb_ref,
                         gbuf, pbuf, cbuf, vbuf, *proj_refs):
    s = pl.program_id(1)
    streams = ((gbuf, HALO_A), (pbuf, HALO_P), (cbuf, HALO_C))

    @pl.when(s == 0)
    def _():
        for buf, halo in streams:
            buf[:, 0:halo, :] = jnp.zeros((N_SUB, halo, D_SUB), F32)

    @pl.when(s > 0)
    def _():
        for buf, halo in streams:
            buf[:, 0:halo, :] = buf[:, TS:TS + halo, :]

    scale = _row_scale(ss_ref)
    sg = sg_ref[...]
    sb = sb_ref[...]
    cab = cab_ref[...]
    lag = lag_ref[...]
    lab = lab_ref[...]
    ps = ps_ref[...]
    rows = lambda c: slice(c * RC, (c + 1) * RC)

    def project(k):
        proj_refs[k][...] = jnp.dot(xs_ref[...], w_ref[:, k * D_GROUP:(k + 1) * D_GROUP],
                                    preferred_element_type=F32) * scale

    def col(k, c, n=RC):
        return proj_refs[k][c * n:(c + 1) * n, :]

    def put(buf, halo, c, val):
        for j in range(N_SUB):
            buf[j, halo + c * RC:halo + (c + 1) * RC, :] = val[:, _group(j)]

    def window(buf, halo, j, c, back):
        lo = halo + c * RC - back
        return buf[j, lo:lo + RC, :]

    def stage_glu(c):
        put(gbuf, HALO_A, c, col(0, c) * jax.nn.sigmoid(col(1, c)))

    def stage_pool(c):
        put(pbuf, HALO_P, c, col(2, c))

    def stage_gated(c):
        put(cbuf, HALO_C, c, col(5, c) * col(3, c))

    def stage_v(c):
        vbuf[rows(c), :] = _layernorm(col(7, c), sg, sb)

    def conv_module(c):
        tiles = []
        for j in range(N_SUB):
            acc = None
            for k in range(CONV_A_W):
                term = caw_ref[k:k + 1, _group(j)] * window(gbuf, HALO_A, j, c, CONV_A_W - 1 - k)
                acc = term if acc is None else acc + term
            tiles.append(acc)
        ya = _layernorm(jnp.concatenate(tiles, axis=-1) + cab, lag, lab)
        ya = ya * jax.nn.sigmoid(ya)
        mix_ref[rows(c), _mixer(0)] = _rms(ya, ong_ref[:, _mixer(0)]).astype(BF16)

    def pooling(c):
        pos = s * TS + c * RC + lax.broadcasted_iota(jnp.int32, (RC, D_SUB), 0)
        yb = []
        for g, w in enumerate(POOL_WINDOWS):
            cur = window(pbuf, HALO_P, g, c, 0)
            tot = cur
            for d in range(1, w):
                tot = tot + window(pbuf, HALO_P, g, c, d)
            if c * RC + 1 >= w:
                pooled = tot / float(w) - cur
            else:
                pooled = tot / jnp.minimum(pos + 1, w).astype(F32) - cur
            yb.append(jnp.dot(pooled.astype(BF16), pw_ref[g].astype(BF16), preferred_element_type=F32))
        yb = jnp.concatenate(yb, axis=-1) * ps
        mix_ref[rows(c), _mixer(1)] = _rms(yb, ong_ref[:, _mixer(1)]).astype(BF16)

    def gated_conv(c):
        tiles = []
        for j in range(N_SUB):
            acc = None
            for k in range(SCONV_W):
                term = scw_ref[k:k + 1, _group(j)] * window(cbuf, HALO_C, j, c, SCONV_W - 1 - k)
                acc = term if acc is None else acc + term
            tiles.append(acc)
        yc = col(4, c) * jnp.concatenate(tiles, axis=-1)
        mix_ref[rows(c), _mixer(2)] = _rms(yc, ong_ref[:, _mixer(2)]).astype(BF16)

    row = lax.broadcasted_iota(jnp.int32, (CHUNK, CHUNK), 0)
    colid = lax.broadcasted_iota(jnp.int32, (CHUNK, CHUNK), 1)
    tril = (row >= colid).astype(F32)
    w_tril = [(sw_ref[g] * tril).astype(BF16) for g in range(N_SUB)]

    def spatial_gate(c):
        r0 = c * CHUNK
        v = vbuf[r0:r0 + CHUNK, :].astype(BF16)
        z = [jnp.dot(w_tril[g], v[:, _group(g)], preferred_element_type=F32) + sbias_ref[:, g:g + 1]
             for g in range(N_SUB)]
        yd = col(6, c, CHUNK) * jnp.concatenate(z, axis=-1)
        mix_ref[r0:r0 + CHUNK, _mixer(3)] = _rms(yd, ong_ref[:, _mixer(3)]).astype(BF16)

    every = lambda f: [functools.partial(f, c) for c in range(TS // RC)]
    convs = every(conv_module)
    gates = [functools.partial(spatial_gate, c) for c in range(TS // CHUNK)]
    plan = [
        (7, []),
        (0, every(stage_v)),
        (1, []),
        (6, every(stage_glu) + convs[0:1]),
        (5, convs[1:2] + gates[0:2]),
        (3, convs[2:3] + gates[2:4]),
        (4, convs[3:4] + every(stage_gated)),
        (2, convs[4:5] + every(gated_conv)),
        (None, convs[5:6] + every(stage_pool) + convs[6:8] + every(pooling)),
    ]
    for k, pieces in plan:
        if k is not None:
            project(k)
        for piece in pieces:
            piece()

    wdb_ref[...] = wd_ref[...].astype(BF16)

    @pl.when(s == pl.num_programs(1) - 1)
    def _():
        def tail(buf, halo, n):
            return jnp.concatenate([buf[j, halo + TS - n:halo + TS, :] for j in range(N_SUB)], axis=-1)

        sta_ref[...] = tail(gbuf, HALO_A, CONV_A_W - 1)
        stp_ref[...] = tail(pbuf, HALO_P, POOL_BUF)
        stc_ref[...] = tail(cbuf, HALO_C, SCONV_W - 1)


def _mixer_prompt(xs, ss, w, w_down, p, layer):
    n_s = SEQ // TS
    wd_rows = D_FF // (BATCH * n_s)

    def lp(shape):
        nd = len(shape)
        return pl.BlockSpec((None,) + shape, lambda b, s: (layer,) + (0,) * nd)

    state = lambda n: pl.BlockSpec((None, n, D_GROUP), lambda b, s: (b, 0, 0))
    return pl.pallas_call(
        _mixer_prompt_kernel,
        grid=(BATCH, n_s),
        in_specs=[
            pl.BlockSpec((TS, D_MODEL), lambda b, s: (b * n_s + s, 0)),
            pl.BlockSpec((TS, 1), lambda b, s: (b * n_s + s, 0)),
            pl.BlockSpec((D_MODEL, D_PROJ), lambda b, s: (0, 0), pipeline_mode=pl.Buffered(1)),
            lp((CONV_A_W, D_GROUP)), lp((1, D_GROUP)), lp((1, D_GROUP)), lp((1, D_GROUP)),
            lp((N_SUB, D_SUB, D_SUB)), lp((1, D_GROUP)), lp((SCONV_W, D_GROUP)),
            lp((1, D_GROUP)), lp((1, D_GROUP)), lp((N_SUB, CHUNK, CHUNK)), lp((CHUNK, N_SUB)),
            lp((1, D_MODEL)),
            pl.BlockSpec((None, wd_rows, D_MODEL), lambda b, s: (layer, b * n_s + s, 0)),
        ],
        out_specs=[
            pl.BlockSpec((TS, D_MODEL), lambda b, s: (b * n_s + s, 0)),
            state(CONV_A_W - 1), state(POOL_BUF), state(SCONV_W - 1),
            pl.BlockSpec((wd_rows, D_MODEL), lambda b, s: (b * n_s + s, 0)),
        ],
        out_shape=[
            jax.ShapeDtypeStruct((M_ALL, D_MODEL), BF16),
            jax.ShapeDtypeStruct((BATCH, CONV_A_W - 1, D_GROUP), F32),
            jax.ShapeDtypeStruct((BATCH, POOL_BUF, D_GROUP), F32),
            jax.ShapeDtypeStruct((BATCH, SCONV_W - 1, D_GROUP), F32),
            jax.ShapeDtypeStruct((D_FF, D_MODEL), BF16),
        ],
        scratch_shapes=[
            pltpu.VMEM((N_SUB, HALO_A + TS, D_SUB), F32),
            pltpu.VMEM((N_SUB, HALO_P + TS, D_SUB), F32),
            pltpu.VMEM((N_SUB, HALO_C + TS, D_SUB), F32),
            pltpu.VMEM((TS, D_GROUP), F32),
        ] + [pltpu.VMEM((TS, D_GROUP), F32)] * N_PROJ,
        compiler_params=_params(("arbitrary", "arbitrary")),
        name=f"mixer_prompt_l{layer}",
    )(xs, ss, w, p["conv_a_w"], p["conv_a_b"], p["ln_a_g"], p["ln_a_b"], p["pool_w"], p["pool_scale"],
      p["sconv_w"], p["sgu_ln_g"], p["sgu_ln_b"], p["sgu_w"], p["sgu_b_t"], p["out_norm_g"], w_down)


N_SAMPLE_IN = 16
N_SAMPLE_OUT = 5


def _mixer_sample_kernel(*refs, layer):
    (wsm_ref, bsm_ref, proj_ref, sa_ref, sp_ref, sc_ref, caw_ref, cab_ref, lag_ref, lab_ref, pw_ref,
     ps_ref, scw_ref, sg_ref, sb_ref, ong_ref) = refs[:N_SAMPLE_IN]
    mix_ref, nsa_ref, nsp_ref, nsc_ref, vn_ref = refs[-N_SAMPLE_OUT:]

    def col(t, k):
        return proj_ref[t, :, k * D_GROUP:(k + 1) * D_GROUP]

    def slab(j):
        return slice(j * D_GROUP, (j + 1) * D_GROUP)

    steps = range(DEC_SEQ)

    glu = [col(t, 0) * jax.nn.sigmoid(col(t, 1)) for t in steps]
    n_a = CONV_A_W - 1

    old_a = [sa_ref[:, j, :] for j in range(n_a)]

    def za(j):
        return old_a[j] if j < n_a else glu[j - n_a]

    cab = cab_ref[...]
    lag = lag_ref[...]
    lab = lab_ref[...]
    for t in steps:
        acc = caw_ref[0:1, :] * za(t)
        for k in range(1, CONV_A_W):
            acc = acc + caw_ref[k:k + 1, :] * za(t + k)
        ya = _layernorm(acc + cab, lag, lab)
        ya = ya * jax.nn.sigmoid(ya)
        mix_ref[t, :, _mixer(0)] = _rms(ya, ong_ref[:, _mixer(0)]).astype(BF16)
    for j in range(n_a):
        nsa_ref[:, j, :] = za(j + DEC_SEQ)

    pin = [col(t, 2) for t in steps]

    old_p = [sp_ref[:, j, :] for j in range(POOL_BUF)]

    def zp(j):
        return old_p[j] if j < POOL_BUF else pin[j - POOL_BUF]

    ps = ps_ref[...]
    for t in steps:
        yb = []
        for g, w in enumerate(POOL_WINDOWS):
            cnt = float(min(PAST_LEN + t + 1, w))
            tot = zp(POOL_BUF + t)[:, _group(g)]
            for d in range(1, w):
                tot = tot + zp(POOL_BUF + t - d)[:, _group(g)]
            pooled = tot / cnt - pin[t][:, _group(g)]
            yb.append(jnp.dot(pooled.astype(BF16), pw_ref[g].astype(BF16), preferred_element_type=F32))
        yb = jnp.concatenate(yb, axis=-1) * ps
        mix_ref[t, :, _mixer(1)] = _rms(yb, ong_ref[:, _mixer(1)]).astype(BF16)
    for j in range(POOL_BUF):
        nsp_ref[:, j, :] = zp(j + DEC_SEQ)

    gated = [col(t, 5) * col(t, 3) for t in steps]
    n_c = SCONV_W - 1

    old_c = [sc_ref[:, j, :] for j in range(n_c)]

    def zc(j):
        return old_c[j] if j < n_c else gated[j - n_c]

    for t in steps:
        conv = scw_ref[0:1, :] * zc(t) + scw_ref[1:2, :] * zc(t + 1) + scw_ref[2:3, :] * zc(t + 2)
        yc = col(t, 4) * conv
        mix_ref[t, :, _mixer(2)] = _rms(yc, ong_ref[:, _mixer(2)]).astype(BF16)
    for j in range(n_c):
        nsc_ref[:, j, :] = zc(j + DEC_SEQ)

    sg = sg_ref[...]
    sb = sb_ref[...]
    vn = [_layernorm(col(t, 7), sg, sb) for t in steps]
    for t in steps:
        vn_ref[:, t, :] = vn[t]
    for t in steps:
        z = []
        for g in range(N_SUB):
            base = (layer * N_SUB + g) * DEC_SEQ * DEC_SEQ + t * DEC_SEQ
            zg = wsm_ref[base] * vn[0][:, _group(g)]
            for j in range(1, t + 1):
                zg = zg + wsm_ref[base + j] * vn[j][:, _group(g)]
            z.append(zg + bsm_ref[(layer * N_SUB + g) * DEC_SEQ + t])
        yd = col(t, 6) * jnp.concatenate(z, axis=-1)
        mix_ref[t, :, _mixer(3)] = _rms(yd, ong_ref[:, _mixer(3)]).astype(BF16)


def _mixer_sample(proj, mix, states, stacked, p, layer):
    sa, sp, sc = states
    slabs = M_ALL // DEC_BATCH
    first = M_P // DEC_BATCH // DEC_SEQ
    proj3 = proj.reshape(DEC_SEQ, DEC_BATCH, D_PROJ)
    mix3 = mix.reshape(slabs, DEC_BATCH, D_MODEL)
    in_place = (mix3,) + tuple(stacked)

    def lp(shape):
        nd = len(shape)
        return pl.BlockSpec((None,) + shape, lambda i: (layer,) + (0,) * nd)

    smem = pl.BlockSpec(memory_space=pltpu.SMEM)
    st = lambda n: pl.BlockSpec((None, BB, n, D_GROUP), lambda i: (layer, i, 0, 0))
    st_shape = lambda n: jax.ShapeDtypeStruct((DEPTH, DEC_BATCH, n, D_GROUP), F32)
    n_a, n_c = CONV_A_W - 1, SCONV_W - 1
    outs = pl.pallas_call(
        functools.partial(_mixer_sample_kernel, layer=layer),
        grid=(DEC_BATCH // BB,),
        in_specs=[
            smem, smem,
            pl.BlockSpec((DEC_SEQ, BB, D_PROJ), lambda i: (0, i, 0)),
            st(n_a), st(POOL_BUF), st(n_c),
            lp((CONV_A_W, D_GROUP)), lp((1, D_GROUP)), lp((1, D_GROUP)), lp((1, D_GROUP)),
            lp((N_SUB, D_SUB, D_SUB)), lp((1, D_GROUP)), lp((SCONV_W, D_GROUP)),
            lp((1, D_GROUP)), lp((1, D_GROUP)), lp((1, D_MODEL)),
        ] + [pl.BlockSpec(memory_space=pl.ANY)] * len(in_place),
        out_specs=[
            pl.BlockSpec((DEC_SEQ, BB, D_MODEL), lambda i: (first, i, 0)),
            st(n_a), st(POOL_BUF), st(n_c), st(DEC_SEQ),
        ],
        out_shape=[
            jax.ShapeDtypeStruct((slabs, DEC_BATCH, D_MODEL), BF16),
            st_shape(n_a), st_shape(POOL_BUF), st_shape(n_c), st_shape(DEC_SEQ),
        ],
        input_output_aliases={N_SAMPLE_IN + k: k for k in range(len(in_place))},
        compiler_params=_params(("arbitrary",)),
        name=f"mixer_sample_l{layer}",
    )(p["sgu_w_small"], p["sgu_b_small"], proj3, sa, sp, sc,
      p["conv_a_w"], p["conv_a_b"], p["ln_a_g"], p["ln_a_b"], p["pool_w"], p["pool_scale"],
      p["sconv_w"], p["sgu_ln_g"], p["sgu_ln_b"], p["out_norm_g"], *in_place)
    return outs[0].reshape(M_ALL, D_MODEL), tuple(outs[1:])


FINAL_ROWS = 512


def _final_norm_kernel(x_ref, ss_ref, g_ref, o_ref):
    o_ref[...] = x_ref[...] * _row_scale(ss_ref) * g_ref[...]


def _final_norm(x, ss, g):
    def call(n_rows, first_block, name):
        rows = lambda c: pl.BlockSpec((FINAL_ROWS, c), lambda i: (first_block + i, 0))
        return pl.pallas_call(
            _final_norm_kernel,
            grid=(n_rows // FINAL_ROWS,),
            in_specs=[rows(D_MODEL), rows(1), pl.BlockSpec((1, D_MODEL), lambda i: (0, 0))],
            out_specs=pl.BlockSpec((FINAL_ROWS, D_MODEL), lambda i: (i, 0)),
            out_shape=jax.ShapeDtypeStruct((n_rows, D_MODEL), F32),
            compiler_params=_params(("arbitrary",)),
            name=name,
        )(x, ss, g)

    y_p = call(M_P, 0, "final_norm_prompt")
    y_s = call(M_S, M_P // FINAL_ROWS, "final_norm_sample")
    return (y_p.reshape(BATCH, SEQ, D_MODEL),
            jnp.swapaxes(y_s.reshape(DEC_SEQ, DEC_BATCH, D_MODEL), 0, 1))


def kernel(x_prompt, x_sample, state_conv_a, state_pool, state_sconv, norm_mix_g, w_in, conv_a_w, conv_a_b, ln_a_g, ln_a_b, pool_w, pool_scale, sconv_w, sgu_ln_g, sgu_ln_b, sgu_w, sgu_b, out_norm_g, w_out, norm_ffn_g, w_gate, w_up, w_down, final_norm_g):
    row = lambda a: a.reshape(DEPTH, 1, a.shape[-1])
    p = {
        "conv_a_w": conv_a_w, "conv_a_b": row(conv_a_b), "ln_a_g": row(ln_a_g), "ln_a_b": row(ln_a_b),
        "pool_w": pool_w, "pool_scale": row(pool_scale), "sconv_w": sconv_w,
        "sgu_ln_g": row(sgu_ln_g), "sgu_ln_b": row(sgu_ln_b), "sgu_w": sgu_w,
        "sgu_b_t": jnp.swapaxes(sgu_b, 1, 2),
        "sgu_w_small": sgu_w[:, :, :DEC_SEQ, :DEC_SEQ].reshape(-1),
        "sgu_b_small": sgu_b[:, :, :DEC_SEQ].reshape(-1),
        "out_norm_g": row(out_norm_g),
    }
    g_mix = row(norm_mix_g)
    g_ffn = row(norm_ffn_g)
    w_in_b = w_in[0].astype(BF16)

    states = (state_conv_a, state_pool, state_sconv)

    x = (x_prompt.reshape(M_P, D_MODEL), jnp.swapaxes(x_sample, 0, 1).reshape(M_S, D_MODEL))
    xs, ss = _prep(*x, g_mix)

    ca_p, pl_p, sc_p = [], [], []
    stacked = ()
    for l in range(DEPTH):
        mix, a_p, p_p, c_p, w_down_b = _mixer_prompt(xs, ss, w_in_b, w_down, p, l)
        proj_s = _in_proj_sample(xs, ss, w_in_b, l)
        mix, stacked = _mixer_sample(proj_s, mix, states, stacked, p, l)
        ca_p.append(a_p); pl_p.append(p_p); sc_p.append(c_p)
        x, xs, ss = _out_proj(mix, w_out, x, g_ffn, l)
        h = _ffn_up(xs, ss, w_gate, w_up, l)
        x, xs, ss, w_in_b = _ffn_down(h, w_down_b, x, g_mix, w_in, l)

    y_prompt, y_sample = _final_norm(x, ss, final_norm_g.reshape(1, D_MODEL))
    ca_s, pl_s, sc_s, v_s = stacked
    return (y_prompt, y_sample, jnp.stack(ca_p), ca_s, jnp.stack(pl_p), pl_s, jnp.stack(sc_p), sc_s, v_s)
```

```python
import functools

import jax
import jax.numpy as jnp
from jax import lax
from jax.experimental import pallas as pl
from jax.experimental.pallas import tpu as pltpu

D_MODEL = 2048
BATCH = 4
SEQ = 2048
DEPTH = 4
DEC_BATCH = 128
DEC_SEQ = 4
PAST_LEN = 16384
D_GROUP = 512
N_SUB = 4
D_SUB = 128
N_PROJ = 8
D_PROJ = N_PROJ * D_GROUP
CONV_A_W = 31
POOL_WINDOWS = (2, 4, 8, 16)
POOL_BUF = 15
SCONV_W = 3
CHUNK = 128
D_FF = 5632
EPS = 1e-6

M_P = BATCH * SEQ
M_S = DEC_BATCH * DEC_SEQ
M_ALL = M_P + M_S

F32 = jnp.float32
BF16 = jnp.bfloat16

V7X_VMEM_LIMIT_BYTES = 56 * 1024 * 1024
V7X_VMEM_LIMIT_BIG_BYTES = 60 * 1024 * 1024

BM = 1088
BM_OUT = 512
BN_IN = 1024
BN_FF = 512
BN_DOWN = 512
CAST_ROWS = 256
PREP_ROWS = 512

BM_FF = 2176

TS = 512
RC = 64
HALO_A = 32
HALO_P = 16
HALO_C = 8

BB = 32


def _params(sem, vmem=V7X_VMEM_LIMIT_BYTES):
    return pltpu.CompilerParams(dimension_semantics=sem, vmem_limit_bytes=vmem)


def _cast_rows_to_bf16(w_ref, wb_ref):
    def body(i, carry):
        r = pl.multiple_of(i * CAST_ROWS, CAST_ROWS)
        wb_ref[pl.ds(r, CAST_ROWS), :] = w_ref[pl.ds(r, CAST_ROWS), :].astype(BF16)
        return carry

    lax.fori_loop(0, w_ref.shape[0] // CAST_ROWS, body, 0)


def _row_scale(ss_ref):
    return lax.rsqrt(ss_ref[...] * (1.0 / D_MODEL) + EPS)


def _two_source_specs(rows):
    n_p = M_P // rows
    return [pl.BlockSpec((rows, D_MODEL), lambda i: (jnp.minimum(i, n_p - 1), 0)),
            pl.BlockSpec((rows, D_MODEL), lambda i: (jnp.maximum(i - n_p, 0), 0))]


def _two_source_tile(xp_ref, xt_ref, rows):
    return jnp.where(pl.program_id(0) < M_P // rows, xp_ref[...], xt_ref[...])


def _prep_kernel(xp_ref, xt_ref, g_ref, xs_ref, ss_ref):
    x = _two_source_tile(xp_ref, xt_ref, PREP_ROWS)
    xs_ref[...] = (x * g_ref[...]).astype(BF16)
    ss_ref[...] = jnp.sum(x * x, axis=-1, keepdims=True)


def _prep(x_prompt, x_sample_tm, g):
    rows = pl.BlockSpec((PREP_ROWS, D_MODEL), lambda i: (i, 0))
    return pl.pallas_call(
        _prep_kernel,
        grid=(M_ALL // PREP_ROWS,),
        in_specs=_two_source_specs(PREP_ROWS) + [pl.BlockSpec((None, 1, D_MODEL), lambda i: (0, 0, 0))],
        out_specs=[rows, pl.BlockSpec((PREP_ROWS, 1), lambda i: (i, 0))],
        out_shape=[
            jax.ShapeDtypeStruct((M_ALL, D_MODEL), BF16),
            jax.ShapeDtypeStruct((M_ALL, 1), F32),
        ],
        compiler_params=_params(("arbitrary",)),
        name="prep",
    )(x_prompt, x_sample_tm, g)


def _in_proj_sample_kernel(xs_ref, ss_ref, w_ref, o_ref):
    o_ref[...] = jnp.dot(xs_ref[...], w_ref[...], preferred_element_type=F32) * _row_scale(ss_ref)


def _in_proj_sample(xs, ss, w, layer):
    first = M_P // M_S
    return pl.pallas_call(
        _in_proj_sample_kernel,
        grid=(D_PROJ // BN_IN,),
        in_specs=[
            pl.BlockSpec((M_S, D_MODEL), lambda j: (first, 0)),
            pl.BlockSpec((M_S, 1), lambda j: (first, 0)),
            pl.BlockSpec((D_MODEL, BN_IN), lambda j: (0, j)),
        ],
        out_specs=pl.BlockSpec((M_S, BN_IN), lambda j: (0, j)),
        out_shape=jax.ShapeDtypeStruct((M_S, D_PROJ), F32),
        compiler_params=_params(("arbitrary",)),
        name=f"in_proj_sample_l{layer}",
    )(xs, ss, w)


def _ffn_up_kernel(xs_ref, ss_ref, wg_ref, wu_ref, o_ref, wgb_ref, wub_ref):
    @pl.when(pl.program_id(1) == 0)
    def _():
        _cast_rows_to_bf16(wg_ref, wgb_ref)
        _cast_rows_to_bf16(wu_ref, wub_ref)

    xs = xs_ref[...]
    scale = _row_scale(ss_ref)
    gate = jnp.dot(xs, wgb_ref[...], preferred_element_type=F32) * scale
    up = jnp.dot(xs, wub_ref[...], preferred_element_type=F32) * scale
    o_ref[...] = (jax.nn.silu(gate) * up).astype(BF16)


def _ffn_up(xs, ss, wg, wu, layer):
    return pl.pallas_call(
        _ffn_up_kernel,
        grid=(D_FF // BN_FF, M_ALL // BM_FF),
        in_specs=[
            pl.BlockSpec((BM_FF, D_MODEL), lambda j, i: (i, 0)),
            pl.BlockSpec((BM_FF, 1), lambda j, i: (i, 0)),
            pl.BlockSpec((None, D_MODEL, BN_FF), lambda j, i: (layer, 0, j)),
            pl.BlockSpec((None, D_MODEL, BN_FF), lambda j, i: (layer, 0, j)),
        ],
        out_specs=pl.BlockSpec((BM_FF, BN_FF), lambda j, i: (i, j)),
        out_shape=jax.ShapeDtypeStruct((M_ALL, D_FF), BF16),
        scratch_shapes=[pltpu.VMEM((D_MODEL, BN_FF), BF16), pltpu.VMEM((D_MODEL, BN_FF), BF16)],
        compiler_params=_params(("arbitrary", "arbitrary"), V7X_VMEM_LIMIT_BIG_BYTES),
        name=f"ffn_up_l{layer}",
    )(xs, ss, wg, wu)


def _out_proj_kernel(ap_ref, at_ref, w_ref, g_ref, *refs):
    *res_refs, x_ref, xs_ref, ss_ref = refs
    a = _two_source_tile(ap_ref, at_ref, BM_OUT)
    res = res_refs[0][...] if len(res_refs) == 1 else _two_source_tile(*res_refs, BM_OUT)
    o = res + jnp.dot(a, w_ref[...], preferred_element_type=F32)
    x_ref[...] = o
    xs_ref[...] = (o * g_ref[...]).astype(BF16)
    ss_ref[...] = jnp.sum(o * o, axis=-1, keepdims=True)


def _out_proj(mix, w, res, g_next, layer):
    rows = pl.BlockSpec((BM_OUT, D_MODEL), lambda i: (i, 0))
    res = res if isinstance(res, tuple) else (res,)
    return pl.pallas_call(
        _out_proj_kernel,
        grid=(M_ALL // BM_OUT,),
        in_specs=_two_source_specs(BM_OUT) + [
            pl.BlockSpec((D_MODEL, D_MODEL), lambda i: (0, 0), pipeline_mode=pl.Buffered(1)),
            pl.BlockSpec((None, 1, D_MODEL), lambda i: (layer, 0, 0)),
        ] + ([rows] if len(res) == 1 else _two_source_specs(BM_OUT)),
        out_specs=[rows, rows, pl.BlockSpec((BM_OUT, 1), lambda i: (i, 0))],
        out_shape=[
            jax.ShapeDtypeStruct((M_ALL, D_MODEL), F32),
            jax.ShapeDtypeStruct((M_ALL, D_MODEL), BF16),
            jax.ShapeDtypeStruct((M_ALL, 1), F32),
        ],
        compiler_params=_params(("arbitrary",)),
        name=f"out_proj_l{layer}",
    )(*mix, w, g_next, *res)


def _ffn_down_kernel(*refs, has_next):
    if has_next:
        h_ref, w_ref, r_ref, g_ref, win_ref, x_ref, xs_ref, ss_ref, winb_ref = refs
        winb_ref[...] = win_ref[...].astype(BF16)
    else:
        h_ref, w_ref, r_ref, x_ref, ss_ref = refs
    o = r_ref[...] + jnp.dot(h_ref[...], w_ref[...], preferred_element_type=F32)
    x_ref[...] = o
    if has_next:
        xs_ref[...] = (o * g_ref[...]).astype(BF16)
    part = jnp.sum(o * o, axis=-1, keepdims=True)
    j = pl.program_id(1)

    @pl.when(j == 0)
    def _():
        ss_ref[...] = part

    @pl.when(j > 0)
    def _():
        ss_ref[...] += part


def _ffn_down(h, w, res, g_next, w_in, layer):
    has_next = layer + 1 < DEPTH
    n_j = D_MODEL // BN_DOWN
    cast_rows = D_MODEL // (M_ALL // BM * n_j)
    tile = pl.BlockSpec((BM, BN_DOWN), lambda i, j: (i, j))
    ss_spec = pl.BlockSpec((BM, 1), lambda i, j: (i, 0))
    x_shape = jax.ShapeDtypeStruct((M_ALL, D_MODEL), F32)
    ss_shape = jax.ShapeDtypeStruct((M_ALL, 1), F32)
    in_specs = [pl.BlockSpec((BM, D_FF), lambda i, j: (i, 0)),
                pl.BlockSpec((D_FF, BN_DOWN), lambda i, j: (0, j)),
                tile]
    args = [h, w, res]
    out_specs, out_shape = [tile, ss_spec], [x_shape, ss_shape]
    if has_next:
        in_specs += [pl.BlockSpec((None, 1, BN_DOWN), lambda i, j: (layer + 1, 0, j)),
                     pl.BlockSpec((None, cast_rows, D_PROJ), lambda i, j: (layer + 1, i * n_j + j, 0))]
        args += [g_next, w_in]
        out_specs = [tile, tile, ss_spec, pl.BlockSpec((cast_rows, D_PROJ), lambda i, j: (i * n_j + j, 0))]
        out_shape = [x_shape, jax.ShapeDtypeStruct((M_ALL, D_MODEL), BF16), ss_shape,
                     jax.ShapeDtypeStruct((D_MODEL, D_PROJ), BF16)]
    outs = pl.pallas_call(
        functools.partial(_ffn_down_kernel, has_next=has_next),
        grid=(M_ALL // BM, n_j),
        in_specs=in_specs,
        out_specs=out_specs,
        out_shape=out_shape,
        compiler_params=_params(("arbitrary", "arbitrary")),
        name=f"ffn_down_l{layer}",
    )(*args)
    return tuple(outs) if has_next else (outs[0], None, outs[1], None)


def _layernorm(x, g, b):
    mu = jnp.mean(x, axis=-1, keepdims=True)
    xc = x - mu
    var = jnp.mean(xc * xc, axis=-1, keepdims=True)
    return xc * lax.rsqrt(var + EPS) * g + b


def _rms(x, g):
    return x * lax.rsqrt(jnp.mean(x * x, axis=-1, keepdims=True) + EPS) * g


def _group(g):
    return slice(g * D_SUB, (g + 1) * D_SUB)


def _mixer(m):
    return slice(m * D_GROUP, (m + 1) * D_GROUP)


def _mixer_prompt_kernel(xs_ref, ss_ref, w_ref, caw_ref, cab_ref, lag_ref, lab_ref, pw_ref, ps_ref, scw_ref,
                         sg_ref, sb_ref, sw_ref, sbias_ref, ong_ref, wd_ref, wo_ref,
                         mix_ref, sta_ref, stp_ref, stc_ref, wdb_ref, wob_ref,
                         gbuf, pbuf, cbuf, vbuf, *proj_refs):
    s = pl.program_id(1)
    streams = ((gbuf, HALO_A), (pbuf, HALO_P), (cbuf, HALO_C))

    @pl.when(s == 0)
    def _():
        for buf, halo in streams:
            buf[:, 0:halo, :] = jnp.zeros((N_SUB, halo, D_SUB), F32)

    @pl.when(s > 0)
    def _():
        for buf, halo in streams:
            buf[:, 0:halo, :] = buf[:, TS:TS + halo, :]

    scale = _row_scale(ss_ref)
    sg = sg_ref[...]
    sb = sb_ref[...]
    cab = cab_ref[...]
    lag = lag_ref[...]
    lab = lab_ref[...]
    ps = ps_ref[...]
    rows = lambda c: slice(c * RC, (c + 1) * RC)

    def project(k):
        proj_refs[k][...] = jnp.dot(xs_ref[...], w_ref[:, k * D_GROUP:(k + 1) * D_GROUP],
                                    preferred_element_type=F32) * scale

    def col(k, c, n=RC):
        return proj_refs[k][c * n:(c + 1) * n, :]

    def put(buf, halo, c, val):
        for j in range(N_SUB):
            buf[j, halo + c * RC:halo + (c + 1) * RC, :] = val[:, _group(j)]

    def window(buf, halo, j, c, back):
        lo = halo + c * RC - back
        return buf[j, lo:lo + RC, :]

    def stage_glu(c):
        put(gbuf, HALO_A, c, col(0, c) * jax.nn.sigmoid(col(1, c)))

    def stage_pool(c):
        put(pbuf, HALO_P, c, col(2, c))

    def stage_gated(c):
        put(cbuf, HALO_C, c, col(5, c) * col(3, c))

    def stage_v(c):
        vbuf[rows(c), :] = _layernorm(col(7, c), sg, sb)

    def conv_module(c):
        tiles = []
        for j in range(N_SUB):
            acc = None
            for k in range(CONV_A_W):
                term = caw_ref[k:k + 1, _group(j)] * window(gbuf, HALO_A, j, c, CONV_A_W - 1 - k)
                acc = term if acc is None else acc + term
            tiles.append(acc)
        ya = _layernorm(jnp.concatenate(tiles, axis=-1) + cab, lag, lab)
        ya = ya * jax.nn.sigmoid(ya)
        mix_ref[rows(c), _mixer(0)] = _rms(ya, ong_ref[:, _mixer(0)]).astype(BF16)

    def pooling(c):
        pos = s * TS + c * RC + lax.broadcasted_iota(jnp.int32, (RC, D_SUB), 0)
        yb = []
        for g, w in enumerate(POOL_WINDOWS):
            cur = window(pbuf, HALO_P, g, c, 0)
            tot = cur
            for d in range(1, w):
                tot = tot + window(pbuf, HALO_P, g, c, d)
            if c * RC + 1 >= w:
                pooled = tot / float(w) - cur
            else:
                pooled = tot / jnp.minimum(pos + 1, w).astype(F32) - cur
            yb.append(jnp.dot(pooled.astype(BF16), pw_ref[g].astype(BF16), preferred_element_type=F32))
        yb = jnp.concatenate(yb, axis=-1) * ps
        mix_ref[rows(c), _mixer(1)] = _rms(yb, ong_ref[:, _mixer(1)]).astype(BF16)

    def gated_conv(c):
        tiles = []
        for j in range(N_SUB):
            acc = None
            for k in range(SCONV_W):
                term = scw_ref[k:k + 1, _group(j)] * window(cbuf, HALO_C, j, c, SCONV_W - 1 - k)
                acc = term if acc is None else acc + term
            tiles.append(acc)
        yc = col(4, c) * jnp.concatenate(tiles, axis=-1)
        mix_ref[rows(c), _mixer(2)] = _rms(yc, ong_ref[:, _mixer(2)]).astype(BF16)

    row = lax.broadcasted_iota(jnp.int32, (CHUNK, CHUNK), 0)
    colid = lax.broadcasted_iota(jnp.int32, (CHUNK, CHUNK), 1)
    tril = (row >= colid).astype(F32)
    w_tril = [(sw_ref[g] * tril).astype(BF16) for g in range(N_SUB)]

    def spatial_gate(c):
        r0 = c * CHUNK
        v = vbuf[r0:r0 + CHUNK, :].astype(BF16)
        z = [jnp.dot(w_tril[g], v[:, _group(g)], preferred_element_type=F32) + sbias_ref[:, g:g + 1]
             for g in range(N_SUB)]
        yd = col(6, c, CHUNK) * jnp.concatenate(z, axis=-1)
        mix_ref[r0:r0 + CHUNK, _mixer(3)] = _rms(yd, ong_ref[:, _mixer(3)]).astype(BF16)

    every = lambda f: [functools.partial(f, c) for c in range(TS // RC)]
    convs = every(conv_module)
    gates = [functools.partial(spatial_gate, c) for c in range(TS // CHUNK)]
    plan = [
        (7, []),
        (0, every(stage_v)),
        (1, []),
        (6, every(stage_glu) + convs[0:1]),
        (5, convs[1:2] + gates[0:2]),
        (3, convs[2:3] + gates[2:4]),
        (4, convs[3:4] + every(stage_gated)),
        (2, convs[4:5] + every(gated_conv)),
        (None, convs[5:6] + every(stage_pool) + convs[6:8] + every(pooling)),
    ]
    for k, pieces in plan:
        if k is not None:
            project(k)
        for piece in pieces:
            piece()

    wdb_ref[...] = wd_ref[...].astype(BF16)
    wob_ref[...] = wo_ref[...].astype(BF16)

    @pl.when(s == pl.num_programs(1) - 1)
    def _():
        def tail(buf, halo, n):
            return jnp.concatenate([buf[j, halo + TS - n:halo + TS, :] for j in range(N_SUB)], axis=-1)

        sta_ref[...] = tail(gbuf, HALO_A, CONV_A_W - 1)
        stp_ref[...] = tail(pbuf, HALO_P, POOL_BUF)
        stc_ref[...] = tail(cbuf, HALO_C, SCONV_W - 1)


def _mixer_prompt(xs, ss, w, w_down, w_out, p, layer):
    n_s = SEQ // TS
    wd_rows = D_FF // (BATCH * n_s)
    wo_rows = D_MODEL // (BATCH * n_s)

    def lp(shape):
        nd = len(shape)
        return pl.BlockSpec((None,) + shape, lambda b, s: (layer,) + (0,) * nd)

    state = lambda n: pl.BlockSpec((None, n, D_GROUP), lambda b, s: (b, 0, 0))
    return pl.pallas_call(
        _mixer_prompt_kernel,
        grid=(BATCH, n_s),
        in_specs=[
            pl.BlockSpec((TS, D_MODEL), lambda b, s: (b * n_s + s, 0)),
            pl.BlockSpec((TS, 1), lambda b, s: (b * n_s + s, 0)),
            pl.BlockSpec((D_MODEL, D_PROJ), lambda b, s: (0, 0), pipeline_mode=pl.Buffered(1)),
            lp((CONV_A_W, D_GROUP)), lp((1, D_GROUP)), lp((1, D_GROUP)), lp((1, D_GROUP)),
            lp((N_SUB, D_SUB, D_SUB)), lp((1, D_GROUP)), lp((SCONV_W, D_GROUP)),
            lp((1, D_GROUP)), lp((1, D_GROUP)), lp((N_SUB, CHUNK, CHUNK)), lp((CHUNK, N_SUB)),
            lp((1, D_MODEL)),
            pl.BlockSpec((None, wd_rows, D_MODEL), lambda b, s: (layer, b * n_s + s, 0)),
            pl.BlockSpec((None, wo_rows, D_MODEL), lambda b, s: (layer, b * n_s + s, 0)),
        ],
        out_specs=[
            pl.BlockSpec((TS, D_MODEL), lambda b, s: (b * n_s + s, 0)),
            state(CONV_A_W - 1), state(POOL_BUF), state(SCONV_W - 1),
            pl.BlockSpec((wd_rows, D_MODEL), lambda b, s: (b * n_s + s, 0)),
            pl.BlockSpec((wo_rows, D_MODEL), lambda b, s: (b * n_s + s, 0)),
        ],
        out_shape=[
            jax.ShapeDtypeStruct((M_P, D_MODEL), BF16),
            jax.ShapeDtypeStruct((BATCH, CONV_A_W - 1, D_GROUP), F32),
            jax.ShapeDtypeStruct((BATCH, POOL_BUF, D_GROUP), F32),
            jax.ShapeDtypeStruct((BATCH, SCONV_W - 1, D_GROUP), F32),
            jax.ShapeDtypeStruct((D_FF, D_MODEL), BF16),
            jax.ShapeDtypeStruct((D_MODEL, D_MODEL), BF16),
        ],
        scratch_shapes=[
            pltpu.VMEM((N_SUB, HALO_A + TS, D_SUB), F32),
            pltpu.VMEM((N_SUB, HALO_P + TS, D_SUB), F32),
            pltpu.VMEM((N_SUB, HALO_C + TS, D_SUB), F32),
            pltpu.VMEM((TS, D_GROUP), F32),
        ] + [pltpu.VMEM((TS, D_GROUP), F32)] * N_PROJ,
        compiler_params=_params(("arbitrary", "arbitrary")),
        name=f"mixer_prompt_l{layer}",
    )(xs, ss, w, p["conv_a_w"], p["conv_a_b"], p["ln_a_g"], p["ln_a_b"], p["pool_w"], p["pool_scale"],
      p["sconv_w"], p["sgu_ln_g"], p["sgu_ln_b"], p["sgu_w"], p["sgu_b_t"], p["out_norm_g"], w_down, w_out)


N_SAMPLE_IN = 16
N_SAMPLE_OUT = 5


def _mixer_sample_kernel(*refs, layer):
    (wsm_ref, bsm_ref, proj_ref, sa_ref, sp_ref, sc_ref, caw_ref, cab_ref, lag_ref, lab_ref, pw_ref,
     ps_ref, scw_ref, sg_ref, sb_ref, ong_ref) = refs[:N_SAMPLE_IN]
    mix_ref, nsa_ref, nsp_ref, nsc_ref, vn_ref = refs[-N_SAMPLE_OUT:]

    def col(t, k):
        return proj_ref[t, :, k * D_GROUP:(k + 1) * D_GROUP]

    def slab(j):
        return slice(j * D_GROUP, (j + 1) * D_GROUP)

    steps = range(DEC_SEQ)

    glu = [col(t, 0) * jax.nn.sigmoid(col(t, 1)) for t in steps]
    n_a = CONV_A_W - 1

    old_a = [sa_ref[j] for j in range(n_a)]

    def za(j):
        return old_a[j] if j < n_a else glu[j - n_a]

    cab = cab_ref[...]
    lag = lag_ref[...]
    lab = lab_ref[...]
    for t in steps:
        acc = caw_ref[0:1, :] * za(t)
        for k in range(1, CONV_A_W):
            acc = acc + caw_ref[k:k + 1, :] * za(t + k)
        ya = _layernorm(acc + cab, lag, lab)
        ya = ya * jax.nn.sigmoid(ya)
        mix_ref[t, :, _mixer(0)] = _rms(ya, ong_ref[:, _mixer(0)]).astype(BF16)
    for j in range(n_a):
        nsa_ref[j] = za(j + DEC_SEQ)

    pin = [col(t, 2) for t in steps]

    old_p = [sp_ref[j] for j in range(POOL_BUF)]

    def zp(j):
        return old_p[j] if j < POOL_BUF else pin[j - POOL_BUF]

    ps = ps_ref[...]
    for t in steps:
        yb = []
        for g, w in enumerate(POOL_WINDOWS):
            cnt = float(min(PAST_LEN + t + 1, w))
            tot = zp(POOL_BUF + t)[:, _group(g)]
            for d in range(1, w):
                tot = tot + zp(POOL_BUF + t - d)[:, _group(g)]
            pooled = tot / cnt - pin[t][:, _group(g)]
            yb.append(jnp.dot(pooled.astype(BF16), pw_ref[g].astype(BF16), preferred_element_type=F32))
        yb = jnp.concatenate(yb, axis=-1) * ps
        mix_ref[t, :, _mixer(1)] = _rms(yb, ong_ref[:, _mixer(1)]).astype(BF16)
    for j in range(POOL_BUF):
        nsp_ref[j] = zp(j + DEC_SEQ)

    gated = [col(t, 5) * col(t, 3) for t in steps]
    n_c = SCONV_W - 1

    old_c = [sc_ref[j] for j in range(n_c)]

    def zc(j):
        return old_c[j] if j < n_c else gated[j - n_c]

    for t in steps:
        conv = scw_ref[0:1, :] * zc(t) + scw_ref[1:2, :] * zc(t + 1) + scw_ref[2:3, :] * zc(t + 2)
        yc = col(t, 4) * conv
        mix_ref[t, :, _mixer(2)] = _rms(yc, ong_ref[:, _mixer(2)]).astype(BF16)
    for j in range(n_c):
        nsc_ref[j] = zc(j + DEC_SEQ)

    sg = sg_ref[...]
    sb = sb_ref[...]
    vn = [_layernorm(col(t, 7), sg, sb) for t in steps]
    for t in steps:
        vn_ref[t] = vn[t]
    for t in steps:
        z = []
        for g in range(N_SUB):
            base = (layer * N_SUB + g) * DEC_SEQ * DEC_SEQ + t * DEC_SEQ
            zg = wsm_ref[base] * vn[0][:, _group(g)]
            for j in range(1, t + 1):
                zg = zg + wsm_ref[base + j] * vn[j][:, _group(g)]
            z.append(zg + bsm_ref[(layer * N_SUB + g) * DEC_SEQ + t])
        yd = col(t, 6) * jnp.concatenate(z, axis=-1)
        mix_ref[t, :, _mixer(3)] = _rms(yd, ong_ref[:, _mixer(3)]).astype(BF16)


def _mixer_sample(proj, states, stacked, p, layer):
    sa, sp, sc = states
    proj3 = proj.reshape(DEC_SEQ, DEC_BATCH, D_PROJ)
    in_place = tuple(stacked)

    def lp(shape):
        nd = len(shape)
        return pl.BlockSpec((None,) + shape, lambda i: (layer,) + (0,) * nd)

    smem = pl.BlockSpec(memory_space=pltpu.SMEM)
    st = lambda n: pl.BlockSpec((None, n, BB, D_GROUP), lambda i: (layer, 0, i, 0))
    st_shape = lambda n: jax.ShapeDtypeStruct((DEPTH, n, DEC_BATCH, D_GROUP), F32)
    n_a, n_c = CONV_A_W - 1, SCONV_W - 1
    outs = pl.pallas_call(
        functools.partial(_mixer_sample_kernel, layer=layer),
        grid=(DEC_BATCH // BB,),
        in_specs=[
            smem, smem,
            pl.BlockSpec((DEC_SEQ, BB, D_PROJ), lambda i: (0, i, 0)),
            st(n_a), st(POOL_BUF), st(n_c),
            lp((CONV_A_W, D_GROUP)), lp((1, D_GROUP)), lp((1, D_GROUP)), lp((1, D_GROUP)),
            lp((N_SUB, D_SUB, D_SUB)), lp((1, D_GROUP)), lp((SCONV_W, D_GROUP)),
            lp((1, D_GROUP)), lp((1, D_GROUP)), lp((1, D_MODEL)),
        ] + [pl.BlockSpec(memory_space=pl.ANY)] * len(in_place),
        out_specs=[
            pl.BlockSpec((DEC_SEQ, BB, D_MODEL), lambda i: (0, i, 0)),
            st(n_a), st(POOL_BUF), st(n_c), st(DEC_SEQ),
        ],
        out_shape=[
            jax.ShapeDtypeStruct((DEC_SEQ, DEC_BATCH, D_MODEL), BF16),
            st_shape(n_a), st_shape(POOL_BUF), st_shape(n_c), st_shape(DEC_SEQ),
        ],
        input_output_aliases={N_SAMPLE_IN + k: k + 1 for k in range(len(in_place))},
        compiler_params=_params(("arbitrary",)),
        name=f"mixer_sample_l{layer}",
    )(p["sgu_w_small"], p["sgu_b_small"], proj3, sa, sp, sc,
      p["conv_a_w"], p["conv_a_b"], p["ln_a_g"], p["ln_a_b"], p["pool_w"], p["pool_scale"],
      p["sconv_w"], p["sgu_ln_g"], p["sgu_ln_b"], p["out_norm_g"], *in_place)
    return outs[0].reshape(M_S, D_MODEL), tuple(outs[1:])


FINAL_ROWS = 512


def _final_norm_kernel(x_ref, ss_ref, g_ref, o_ref):
    o_ref[...] = x_ref[...] * _row_scale(ss_ref) * g_ref[...]


def _final_norm(x, ss, g):
    def call(n_rows, first_block, name):
        rows = lambda c: pl.BlockSpec((FINAL_ROWS, c), lambda i: (first_block + i, 0))
        return pl.pallas_call(
            _final_norm_kernel,
            grid=(n_rows // FINAL_ROWS,),
            in_specs=[rows(D_MODEL), rows(1), pl.BlockSpec((1, D_MODEL), lambda i: (0, 0))],
            out_specs=pl.BlockSpec((FINAL_ROWS, D_MODEL), lambda i: (i, 0)),
            out_shape=jax.ShapeDtypeStruct((n_rows, D_MODEL), F32),
            compiler_params=_params(("arbitrary",)),
            name=name,
        )(x, ss, g)

    y_p = call(M_P, 0, "final_norm_prompt")
    y_s = call(M_S, M_P // FINAL_ROWS, "final_norm_sample")
    return (y_p.reshape(BATCH, SEQ, D_MODEL),
            jnp.swapaxes(y_s.reshape(DEC_SEQ, DEC_BATCH, D_MODEL), 0, 1))


def kernel(x_prompt, x_sample, state_conv_a, state_pool, state_sconv, norm_mix_g, w_in, conv_a_w, conv_a_b, ln_a_g, ln_a_b, pool_w, pool_scale, sconv_w, sgu_ln_g, sgu_ln_b, sgu_w, sgu_b, out_norm_g, w_out, norm_ffn_g, w_gate, w_up, w_down, final_norm_g):
    row = lambda a: a.reshape(DEPTH, 1, a.shape[-1])
    p = {
        "conv_a_w": conv_a_w, "conv_a_b": row(conv_a_b), "ln_a_g": row(ln_a_g), "ln_a_b": row(ln_a_b),
        "pool_w": pool_w, "pool_scale": row(pool_scale), "sconv_w": sconv_w,
        "sgu_ln_g": row(sgu_ln_g), "sgu_ln_b": row(sgu_ln_b), "sgu_w": sgu_w,
        "sgu_b_t": jnp.swapaxes(sgu_b, 1, 2),
        "sgu_w_small": sgu_w[:, :, :DEC_SEQ, :DEC_SEQ].reshape(-1),
        "sgu_b_small": sgu_b[:, :, :DEC_SEQ].reshape(-1),
        "out_norm_g": row(out_norm_g),
    }
    g_mix = row(norm_mix_g)
    g_ffn = row(norm_ffn_g)
    w_in_b = w_in[0].astype(BF16)

    rows_major = lambda a: jnp.swapaxes(a, 1, 2)
    states = (rows_major(state_conv_a), rows_major(state_pool), rows_major(state_sconv))

    x = (x_prompt.reshape(M_P, D_MODEL), jnp.swapaxes(x_sample, 0, 1).reshape(M_S, D_MODEL))
    xs, ss = _prep(*x, g_mix)

    ca_p, pl_p, sc_p = [], [], []
    stacked = ()
    for l in range(DEPTH):
        mix_p, a_p, p_p, c_p, w_down_b, w_out_b = _mixer_prompt(xs, ss, w_in_b, w_down, w_out, p, l)
        proj_s = _in_proj_sample(xs, ss, w_in_b, l)
        mix_s, stacked = _mixer_sample(proj_s, states, stacked, p, l)
        ca_p.append(a_p); pl_p.append(p_p); sc_p.append(c_p)
        x, xs, ss = _out_proj((mix_p, mix_s), w_out_b, x, g_ffn, l)
        h = _ffn_up(xs, ss, w_gate, w_up, l)
        x, xs, ss, w_in_b = _ffn_down(h, w_down_b, x, g_mix, w_in, l)

    y_prompt, y_sample = _final_norm(x, ss, final_norm_g.reshape(1, D_MODEL))
    ca_s, pl_s, sc_s, v_s = (rows_major(a) for a in stacked)
    return (y_prompt, y_sample, jnp.stack(ca_p), ca_s, jnp.stack(pl_p), pl_s, jnp.stack(sc_p), sc_s, v_s)
```

```python
import functools

import jax
import jax.numpy as jnp
from jax import lax
from jax.experimental import pallas as pl
from jax.experimental.pallas import tpu as pltpu

D_MODEL = 2048
BATCH = 4
SEQ = 2048
DEPTH = 4
DEC_BATCH = 128
DEC_SEQ = 4
PAST_LEN = 16384
D_GROUP = 512
N_SUB = 4
D_SUB = 128
N_PROJ = 8
D_PROJ = N_PROJ * D_GROUP
CONV_A_W = 31
POOL_WINDOWS = (2, 4, 8, 16)
POOL_BUF = 15
SCONV_W = 3
CHUNK = 128
D_FF = 5632
EPS = 1e-6

M_P = BATCH * SEQ
M_S = DEC_BATCH * DEC_SEQ
M_ALL = M_P + M_S

F32 = jnp.float32
BF16 = jnp.bfloat16

V7X_VMEM_LIMIT_BYTES = 56 * 1024 * 1024
V7X_VMEM_LIMIT_BIG_BYTES = 60 * 1024 * 1024

BM = 1088
BM_OUT = 512
BN_IN = 1024
BN_FF = 512
BN_DOWN = 512
CAST_ROWS = 256
PREP_ROWS = 512

BM_FF = 2176

TS = 512
RC = 64
HALO_A = 32
HALO_P = 16
HALO_C = 8

BB = 32


def _params(sem, vmem=V7X_VMEM_LIMIT_BYTES):
    return pltpu.CompilerParams(dimension_semantics=sem, vmem_limit_bytes=vmem)


def _cast_rows_to_bf16(w_ref, wb_ref):
    def body(i, carry):
        r = pl.multiple_of(i * CAST_ROWS, CAST_ROWS)
        wb_ref[pl.ds(r, CAST_ROWS), :] = w_ref[pl.ds(r, CAST_ROWS), :].astype(BF16)
        return carry

    lax.fori_loop(0, w_ref.shape[0] // CAST_ROWS, body, 0)


def _row_scale(ss_ref):
    return lax.rsqrt(ss_ref[...] * (1.0 / D_MODEL) + EPS)


def _sigmoid(x):
    return 0.5 * jnp.tanh(0.5 * x) + 0.5


def _two_source_specs(rows):
    n_p = M_P // rows
    return [pl.BlockSpec((rows, D_MODEL), lambda i: (jnp.minimum(i, n_p - 1), 0)),
            pl.BlockSpec((rows, D_MODEL), lambda i: (jnp.maximum(i - n_p, 0), 0))]


def _two_source_tile(xp_ref, xt_ref, rows):
    return jnp.where(pl.program_id(0) < M_P // rows, xp_ref[...], xt_ref[...])


def _prep_kernel(xp_ref, xt_ref, g_ref, xs_ref, ss_ref):
    x = _two_source_tile(xp_ref, xt_ref, PREP_ROWS)
    xs_ref[...] = (x * g_ref[...]).astype(BF16)
    ss_ref[...] = jnp.sum(x * x, axis=-1, keepdims=True)


def _prep(x_prompt, x_sample_tm, g):
    rows = pl.BlockSpec((PREP_ROWS, D_MODEL), lambda i: (i, 0))
    return pl.pallas_call(
        _prep_kernel,
        grid=(M_ALL // PREP_ROWS,),
        in_specs=_two_source_specs(PREP_ROWS) + [pl.BlockSpec((None, 1, D_MODEL), lambda i: (0, 0, 0))],
        out_specs=[rows, pl.BlockSpec((PREP_ROWS, 1), lambda i: (i, 0))],
        out_shape=[
            jax.ShapeDtypeStruct((M_ALL, D_MODEL), BF16),
            jax.ShapeDtypeStruct((M_ALL, 1), F32),
        ],
        compiler_params=_params(("arbitrary",)),
        name="prep",
    )(x_prompt, x_sample_tm, g)


def _in_proj_sample_kernel(xs_ref, ss_ref, w_ref, o_ref):
    o_ref[...] = jnp.dot(xs_ref[...], w_ref[...], preferred_element_type=F32) * _row_scale(ss_ref)


def _in_proj_sample(xs, ss, w, layer):
    first = M_P // M_S
    return pl.pallas_call(
        _in_proj_sample_kernel,
        grid=(D_PROJ // BN_IN,),
        in_specs=[
            pl.BlockSpec((M_S, D_MODEL), lambda j: (first, 0)),
            pl.BlockSpec((M_S, 1), lambda j: (first, 0)),
            pl.BlockSpec((D_MODEL, BN_IN), lambda j: (0, j)),
        ],
        out_specs=pl.BlockSpec((M_S, BN_IN), lambda j: (0, j)),
        out_shape=jax.ShapeDtypeStruct((M_S, D_PROJ), F32),
        compiler_params=_params(("arbitrary",)),
        name=f"in_proj_sample_l{layer}",
    )(xs, ss, w)


def _ffn_up_kernel(xs_ref, ss_ref, wg_ref, wu_ref, o_ref, wgb_ref, wub_ref):
    @pl.when(pl.program_id(1) == 0)
    def _():
        _cast_rows_to_bf16(wg_ref, wgb_ref)
        _cast_rows_to_bf16(wu_ref, wub_ref)

    scale = _row_scale(ss_ref)
    gate = jnp.dot(xs_ref[...], wgb_ref[...], preferred_element_type=F32) * scale
    up = jnp.dot(xs_ref[...], wub_ref[...], preferred_element_type=F32) * scale
    o_ref[...] = (gate * _sigmoid(gate) * up).astype(BF16)


def _ffn_up(xs, ss, wg, wu, layer):
    return pl.pallas_call(
        _ffn_up_kernel,
        grid=(D_FF // BN_FF, M_ALL // BM_FF),
        in_specs=[
            pl.BlockSpec((BM_FF, D_MODEL), lambda j, i: (i, 0)),
            pl.BlockSpec((BM_FF, 1), lambda j, i: (i, 0)),
            pl.BlockSpec((None, D_MODEL, BN_FF), lambda j, i: (layer, 0, j)),
            pl.BlockSpec((None, D_MODEL, BN_FF), lambda j, i: (layer, 0, j)),
        ],
        out_specs=pl.BlockSpec((BM_FF, BN_FF), lambda j, i: (i, j)),
        out_shape=jax.ShapeDtypeStruct((M_ALL, D_FF), BF16),
        scratch_shapes=[pltpu.VMEM((D_MODEL, BN_FF), BF16), pltpu.VMEM((D_MODEL, BN_FF), BF16)],
        compiler_params=_params(("arbitrary", "arbitrary"), V7X_VMEM_LIMIT_BIG_BYTES),
        name=f"ffn_up_l{layer}",
    )(xs, ss, wg, wu)


def _out_proj_kernel(ap_ref, at_ref, w_ref, g_ref, *refs):
    *res_refs, x_ref, xs_ref, ss_ref = refs
    a = _two_source_tile(ap_ref, at_ref, BM_OUT)
    res = res_refs[0][...] if len(res_refs) == 1 else _two_source_tile(*res_refs, BM_OUT)
    o = res + jnp.dot(a, w_ref[...], preferred_element_type=F32)
    x_ref[...] = o
    xs_ref[...] = (o * g_ref[...]).astype(BF16)
    ss_ref[...] = jnp.sum(o * o, axis=-1, keepdims=True)


def _out_proj(mix, w, res, g_next, layer):
    rows = pl.BlockSpec((BM_OUT, D_MODEL), lambda i: (i, 0))
    res = res if isinstance(res, tuple) else (res,)
    return pl.pallas_call(
        _out_proj_kernel,
        grid=(M_ALL // BM_OUT,),
        in_specs=_two_source_specs(BM_OUT) + [
            pl.BlockSpec((D_MODEL, D_MODEL), lambda i: (0, 0), pipeline_mode=pl.Buffered(1)),
            pl.BlockSpec((None, 1, D_MODEL), lambda i: (layer, 0, 0)),
        ] + ([rows] if len(res) == 1 else _two_source_specs(BM_OUT)),
        out_specs=[rows, rows, pl.BlockSpec((BM_OUT, 1), lambda i: (i, 0))],
        out_shape=[
            jax.ShapeDtypeStruct((M_ALL, D_MODEL), F32),
            jax.ShapeDtypeStruct((M_ALL, D_MODEL), BF16),
            jax.ShapeDtypeStruct((M_ALL, 1), F32),
        ],
        compiler_params=_params(("arbitrary",)),
        name=f"out_proj_l{layer}",
    )(*mix, w, g_next, *res)


def _ffn_down_kernel(*refs, has_next):
    if has_next:
        h_ref, w_ref, r_ref, g_ref, x_ref, xs_ref, ss_ref = refs
    else:
        h_ref, w_ref, r_ref, x_ref, ss_ref = refs
    o = r_ref[...] + jnp.dot(h_ref[...], w_ref[...], preferred_element_type=F32)
    x_ref[...] = o
    if has_next:
        xs_ref[...] = (o * g_ref[...]).astype(BF16)
    part = jnp.sum(o * o, axis=-1, keepdims=True)
    j = pl.program_id(1)

    @pl.when(j == 0)
    def _():
        ss_ref[...] = part

    @pl.when(j > 0)
    def _():
        ss_ref[...] += part


def _ffn_down(h, w, res, g_next, layer):
    has_next = layer + 1 < DEPTH
    tile = pl.BlockSpec((BM, BN_DOWN), lambda i, j: (i, j))
    ss_spec = pl.BlockSpec((BM, 1), lambda i, j: (i, 0))
    x_shape = jax.ShapeDtypeStruct((M_ALL, D_MODEL), F32)
    ss_shape = jax.ShapeDtypeStruct((M_ALL, 1), F32)
    in_specs = [pl.BlockSpec((BM, D_FF), lambda i, j: (i, 0)),
                pl.BlockSpec((D_FF, BN_DOWN), lambda i, j: (0, j)),
                tile]
    args = [h, w, res]
    out_specs, out_shape = [tile, ss_spec], [x_shape, ss_shape]
    if has_next:
        in_specs.append(pl.BlockSpec((None, 1, BN_DOWN), lambda i, j: (layer + 1, 0, j)))
        args.append(g_next)
        out_specs = [tile, tile, ss_spec]
        out_shape = [x_shape, jax.ShapeDtypeStruct((M_ALL, D_MODEL), BF16), ss_shape]
    outs = pl.pallas_call(
        functools.partial(_ffn_down_kernel, has_next=has_next),
        grid=(M_ALL // BM, D_MODEL // BN_DOWN),
        in_specs=in_specs,
        out_specs=out_specs,
        out_shape=out_shape,
        compiler_params=_params(("arbitrary", "arbitrary")),
        name=f"ffn_down_l{layer}",
    )(*args)
    return tuple(outs) if has_next else (outs[0], None, outs[1])


def _layernorm(x, g, b):
    mu = jnp.mean(x, axis=-1, keepdims=True)
    xc = x - mu
    var = jnp.mean(xc * xc, axis=-1, keepdims=True)
    return xc * lax.rsqrt(var + EPS) * g + b


def _rms(x, g):
    return x * lax.rsqrt(jnp.mean(x * x, axis=-1, keepdims=True) + EPS) * g


def _group(g):
    return slice(g * D_SUB, (g + 1) * D_SUB)


def _mixer(m):
    return slice(m * D_GROUP, (m + 1) * D_GROUP)


N_MIXER_IN = 15


def _mixer_prompt_kernel(*refs, n_cast):
    (xs_ref, ss_ref, w_ref, caw_ref, cab_ref, lag_ref, lab_ref, pw_ref, ps_ref, scw_ref,
     sg_ref, sb_ref, sw_ref, sbias_ref, ong_ref) = refs[:N_MIXER_IN]
    cast_in = refs[N_MIXER_IN:N_MIXER_IN + n_cast]
    mix_ref, sta_ref, stp_ref, stc_ref = refs[N_MIXER_IN + n_cast:N_MIXER_IN + n_cast + 4]
    cast_out = refs[N_MIXER_IN + n_cast + 4:N_MIXER_IN + 2 * n_cast + 4]
    gbuf, pbuf, cbuf, vbuf, *proj_refs = refs[N_MIXER_IN + 2 * n_cast + 4:]
    s = pl.program_id(1)
    streams = ((gbuf, HALO_A), (pbuf, HALO_P), (cbuf, HALO_C))

    @pl.when(s == 0)
    def _():
        for buf, halo in streams:
            buf[:, 0:halo, :] = jnp.zeros((N_SUB, halo, D_SUB), F32)

    @pl.when(s > 0)
    def _():
        for buf, halo in streams:
            buf[:, 0:halo, :] = buf[:, TS:TS + halo, :]

    scale = _row_scale(ss_ref)
    sg = sg_ref[...]
    sb = sb_ref[...]
    cab = cab_ref[...]
    lag = lag_ref[...]
    lab = lab_ref[...]
    ps = ps_ref[...]
    rows = lambda c: slice(c * RC, (c + 1) * RC)

    def project(k):
        proj_refs[k][...] = jnp.dot(xs_ref[...], w_ref[:, k * D_GROUP:(k + 1) * D_GROUP],
                                    preferred_element_type=F32) * scale

    def col(k, c, n=RC):
        return proj_refs[k][c * n:(c + 1) * n, :]

    def put(buf, halo, c, val):
        for j in range(N_SUB):
            buf[j, halo + c * RC:halo + (c + 1) * RC, :] = val[:, _group(j)]

    def window(buf, halo, j, c, back):
        lo = halo + c * RC - back
        return buf[j, lo:lo + RC, :]

    def stage_glu(c):
        put(gbuf, HALO_A, c, col(0, c) * _sigmoid(col(1, c)))

    def stage_pool(c):
        put(pbuf, HALO_P, c, col(2, c))

    def stage_gated(c):
        put(cbuf, HALO_C, c, col(5, c) * col(3, c))

    def stage_v(c):
        vbuf[rows(c), :] = _layernorm(col(7, c), sg, sb)

    def conv_module(c):
        tiles = []
        for j in range(N_SUB):
            acc = None
            for k in range(CONV_A_W):
                term = caw_ref[k:k + 1, _group(j)] * window(gbuf, HALO_A, j, c, CONV_A_W - 1 - k)
                acc = term if acc is None else acc + term
            tiles.append(acc)
        ya = _layernorm(jnp.concatenate(tiles, axis=-1) + cab, lag, lab)
        ya = ya * _sigmoid(ya)
        mix_ref[rows(c), _mixer(0)] = _rms(ya, ong_ref[:, _mixer(0)]).astype(BF16)

    def pooling(c):
        pos = s * TS + c * RC + lax.broadcasted_iota(jnp.int32, (RC, D_SUB), 0)
        yb = []
        for g, w in enumerate(POOL_WINDOWS):
            cur = window(pbuf, HALO_P, g, c, 0)
            tot = cur
            for d in range(1, w):
                tot = tot + window(pbuf, HALO_P, g, c, d)
            if c * RC + 1 >= w:
                pooled = tot / float(w) - cur
            else:
                pooled = tot / jnp.minimum(pos + 1, w).astype(F32) - cur
            yb.append(jnp.dot(pooled.astype(BF16), pw_ref[g].astype(BF16), preferred_element_type=F32))
        yb = jnp.concatenate(yb, axis=-1) * ps
        mix_ref[rows(c), _mixer(1)] = _rms(yb, ong_ref[:, _mixer(1)]).astype(BF16)

    def gated_conv(c):
        tiles = []
        for j in range(N_SUB):
            acc = None
            for k in range(SCONV_W):
                term = scw_ref[k:k + 1, _group(j)] * window(cbuf, HALO_C, j, c, SCONV_W - 1 - k)
                acc = term if acc is None else acc + term
            tiles.append(acc)
        yc = col(4, c) * jnp.concatenate(tiles, axis=-1)
        mix_ref[rows(c), _mixer(2)] = _rms(yc, ong_ref[:, _mixer(2)]).astype(BF16)

    row = lax.broadcasted_iota(jnp.int32, (CHUNK, CHUNK), 0)
    colid = lax.broadcasted_iota(jnp.int32, (CHUNK, CHUNK), 1)
    tril = (row >= colid).astype(F32)
    w_tril = [(sw_ref[g] * tril).astype(BF16) for g in range(N_SUB)]

    def spatial_gate(c):
        r0 = c * CHUNK
        v = vbuf[r0:r0 + CHUNK, :].astype(BF16)
        z = [jnp.dot(w_tril[g], v[:, _group(g)], preferred_element_type=F32) + sbias_ref[:, g:g + 1]
             for g in range(N_SUB)]
        yd = col(6, c, CHUNK) * jnp.concatenate(z, axis=-1)
        mix_ref[r0:r0 + CHUNK, _mixer(3)] = _rms(yd, ong_ref[:, _mixer(3)]).astype(BF16)

    every = lambda f: [functools.partial(f, c) for c in range(TS // RC)]
    convs = every(conv_module)
    gates = [functools.partial(spatial_gate, c) for c in range(TS // CHUNK)]
    plan = [
        (7, []),
        (0, every(stage_v)),
        (1, []),
        (6, every(stage_glu) + convs[0:1]),
        (5, convs[1:2] + gates[0:2]),
        (3, convs[2:3] + gates[2:4]),
        (4, convs[3:4] + every(stage_gated)),
        (2, convs[4:5] + every(gated_conv)),
        (None, convs[5:6] + every(stage_pool) + convs[6:8] + every(pooling)),
    ]
    for k, pieces in plan:
        if k is not None:
            project(k)
        for piece in pieces:
            piece()

    for src, dst in zip(cast_in, cast_out):
        dst[...] = src[...].astype(BF16)

    @pl.when(s == pl.num_programs(1) - 1)
    def _():
        def tail(buf, halo, n):
            return jnp.concatenate([buf[j, halo + TS - n:halo + TS, :] for j in range(N_SUB)], axis=-1)

        sta_ref[...] = tail(gbuf, HALO_A, CONV_A_W - 1)
        stp_ref[...] = tail(pbuf, HALO_P, POOL_BUF)
        stc_ref[...] = tail(cbuf, HALO_C, SCONV_W - 1)


def _mixer_prompt(xs, ss, w, to_round, p, layer):
    n_s = SEQ // TS
    steps = BATCH * n_s

    def lp(shape):
        nd = len(shape)
        return pl.BlockSpec((None,) + shape, lambda b, s: (layer,) + (0,) * nd)

    def cast_in(arr, l):
        return pl.BlockSpec((None, arr.shape[1] // steps, arr.shape[2]), lambda b, s: (l, b * n_s + s, 0))

    def cast_out(arr):
        return pl.BlockSpec((arr.shape[1] // steps, arr.shape[2]), lambda b, s: (b * n_s + s, 0))

    state = lambda n: pl.BlockSpec((None, n, D_GROUP), lambda b, s: (b, 0, 0))
    return pl.pallas_call(
        functools.partial(_mixer_prompt_kernel, n_cast=len(to_round)),
        grid=(BATCH, n_s),
        in_specs=[
            pl.BlockSpec((TS, D_MODEL), lambda b, s: (b * n_s + s, 0)),
            pl.BlockSpec((TS, 1), lambda b, s: (b * n_s + s, 0)),
            pl.BlockSpec((D_MODEL, D_PROJ), lambda b, s: (0, 0), pipeline_mode=pl.Buffered(1)),
            lp((CONV_A_W, D_GROUP)), lp((1, D_GROUP)), lp((1, D_GROUP)), lp((1, D_GROUP)),
            lp((N_SUB, D_SUB, D_SUB)), lp((1, D_GROUP)), lp((SCONV_W, D_GROUP)),
            lp((1, D_GROUP)), lp((1, D_GROUP)), lp((N_SUB, CHUNK, CHUNK)), lp((CHUNK, N_SUB)),
            lp((1, D_MODEL)),
        ] + [cast_in(arr, l) for arr, l in to_round],
        out_specs=[
            pl.BlockSpec((TS, D_MODEL), lambda b, s: (b * n_s + s, 0)),
            state(CONV_A_W - 1), state(POOL_BUF), state(SCONV_W - 1),
        ] + [cast_out(arr) for arr, _ in to_round],
        out_shape=[
            jax.ShapeDtypeStruct((M_P, D_MODEL), BF16),
            jax.ShapeDtypeStruct((BATCH, CONV_A_W - 1, D_GROUP), F32),
            jax.ShapeDtypeStruct((BATCH, POOL_BUF, D_GROUP), F32),
            jax.ShapeDtypeStruct((BATCH, SCONV_W - 1, D_GROUP), F32),
        ] + [jax.ShapeDtypeStruct(arr.shape[1:], BF16) for arr, _ in to_round],
        scratch_shapes=[
            pltpu.VMEM((N_SUB, HALO_A + TS, D_SUB), F32),
            pltpu.VMEM((N_SUB, HALO_P + TS, D_SUB), F32),
            pltpu.VMEM((N_SUB, HALO_C + TS, D_SUB), F32),
            pltpu.VMEM((TS, D_GROUP), F32),
        ] + [pltpu.VMEM((TS, D_GROUP), F32)] * N_PROJ,
        compiler_params=_params(("arbitrary", "arbitrary")),
        name=f"mixer_prompt_l{layer}",
    )(xs, ss, w, p["conv_a_w"], p["conv_a_b"], p["ln_a_g"], p["ln_a_b"], p["pool_w"], p["pool_scale"],
      p["sconv_w"], p["sgu_ln_g"], p["sgu_ln_b"], p["sgu_w"], p["sgu_b_t"], p["out_norm_g"],
      *[arr for arr, _ in to_round])


N_SAMPLE_IN = 16
N_SAMPLE_OUT = 5


def _mixer_sample_kernel(*refs, layer):
    (wsm_ref, bsm_ref, proj_ref, sa_ref, sp_ref, sc_ref, caw_ref, cab_ref, lag_ref, lab_ref, pw_ref,
     ps_ref, scw_ref, sg_ref, sb_ref, ong_ref) = refs[:N_SAMPLE_IN]
    mix_ref, nsa_ref, nsp_ref, nsc_ref, vn_ref = refs[-N_SAMPLE_OUT:]

    def col(t, k):
        return proj_ref[t, :, k * D_GROUP:(k + 1) * D_GROUP]

    def slab(j):
        return slice(j * D_GROUP, (j + 1) * D_GROUP)

    steps = range(DEC_SEQ)

    glu = [col(t, 0) * _sigmoid(col(t, 1)) for t in steps]
    n_a = CONV_A_W - 1

    old_a = [sa_ref[j] for j in range(n_a)]

    def za(j):
        return old_a[j] if j < n_a else glu[j - n_a]

    cab = cab_ref[...]
    lag = lag_ref[...]
    lab = lab_ref[...]
    for t in steps:
        acc = caw_ref[0:1, :] * za(t)
        for k in range(1, CONV_A_W):
            acc = acc + caw_ref[k:k + 1, :] * za(t + k)
        ya = _layernorm(acc + cab, lag, lab)
        ya = ya * _sigmoid(ya)
        mix_ref[t, :, _mixer(0)] = _rms(ya, ong_ref[:, _mixer(0)]).astype(BF16)
    for j in range(n_a):
        nsa_ref[j] = za(j + DEC_SEQ)

    pin = [col(t, 2) for t in steps]

    old_p = [sp_ref[j] for j in range(POOL_BUF)]

    def zp(j):
        return old_p[j] if j < POOL_BUF else pin[j - POOL_BUF]

    ps = ps_ref[...]
    for t in steps:
        yb = []
        for g, w in enumerate(POOL_WINDOWS):
            cnt = float(min(PAST_LEN + t + 1, w))
            tot = zp(POOL_BUF + t)[:, _group(g)]
            for d in range(1, w):
                tot = tot + zp(POOL_BUF + t - d)[:, _group(g)]
            pooled = tot / cnt - pin[t][:, _group(g)]
            yb.append(jnp.dot(pooled.astype(BF16), pw_ref[g].astype(BF16), preferred_element_type=F32))
        yb = jnp.concatenate(yb, axis=-1) * ps
        mix_ref[t, :, _mixer(1)] = _rms(yb, ong_ref[:, _mixer(1)]).astype(BF16)
    for j in range(POOL_BUF):
        nsp_ref[j] = zp(j + DEC_SEQ)

    gated = [col(t, 5) * col(t, 3) for t in steps]
    n_c = SCONV_W - 1

    old_c = [sc_ref[j] for j in range(n_c)]

    def zc(j):
        return old_c[j] if j < n_c else gated[j - n_c]

    for t in steps:
        conv = scw_ref[0:1, :] * zc(t) + scw_ref[1:2, :] * zc(t + 1) + scw_ref[2:3, :] * zc(t + 2)
        yc = col(t, 4) * conv
        mix_ref[t, :, _mixer(2)] = _rms(yc, ong_ref[:, _mixer(2)]).astype(BF16)
    for j in range(n_c):
        nsc_ref[j] = zc(j + DEC_SEQ)

    sg = sg_ref[...]
    sb = sb_ref[...]
    vn = [_layernorm(col(t, 7), sg, sb) for t in steps]
    for t in steps:
        vn_ref[t] = vn[t]
    for t in steps:
        z = []
        for g in range(N_SUB):
            base = (layer * N_SUB + g) * DEC_SEQ * DEC_SEQ + t * DEC_SEQ
            zg = wsm_ref[base] * vn[0][:, _group(g)]
            for j in range(1, t + 1):
                zg = zg + wsm_ref[base + j] * vn[j][:, _group(g)]
            z.append(zg + bsm_ref[(layer * N_SUB + g) * DEC_SEQ + t])
        yd = col(t, 6) * jnp.concatenate(z, axis=-1)
        mix_ref[t, :, _mixer(3)] = _rms(yd, ong_ref[:, _mixer(3)]).astype(BF16)


def _mixer_sample(proj, states, stacked, p, layer):
    sa, sp, sc = states
    proj3 = proj.reshape(DEC_SEQ, DEC_BATCH, D_PROJ)
    in_place = tuple(stacked)

    def lp(shape):
        nd = len(shape)
        return pl.BlockSpec((None,) + shape, lambda i: (layer,) + (0,) * nd)

    smem = pl.BlockSpec(memory_space=pltpu.SMEM)
    st = lambda n: pl.BlockSpec((None, n, BB, D_GROUP), lambda i: (layer, 0, i, 0))
    st_shape = lambda n: jax.ShapeDtypeStruct((DEPTH, n, DEC_BATCH, D_GROUP), F32)
    n_a, n_c = CONV_A_W - 1, SCONV_W - 1
    outs = pl.pallas_call(
        functools.partial(_mixer_sample_kernel, layer=layer),
        grid=(DEC_BATCH // BB,),
        in_specs=[
            smem, smem,
            pl.BlockSpec((DEC_SEQ, BB, D_PROJ), lambda i: (0, i, 0)),
            st(n_a), st(POOL_BUF), st(n_c),
            lp((CONV_A_W, D_GROUP)), lp((1, D_GROUP)), lp((1, D_GROUP)), lp((1, D_GROUP)),
            lp((N_SUB, D_SUB, D_SUB)), lp((1, D_GROUP)), lp((SCONV_W, D_GROUP)),
            lp((1, D_GROUP)), lp((1, D_GROUP)), lp((1, D_MODEL)),
        ] + [pl.BlockSpec(memory_space=pl.ANY)] * len(in_place),
        out_specs=[
            pl.BlockSpec((DEC_SEQ, BB, D_MODEL), lambda i: (0, i, 0)),
            st(n_a), st(POOL_BUF), st(n_c), st(DEC_SEQ),
        ],
        out_shape=[
            jax.ShapeDtypeStruct((DEC_SEQ, DEC_BATCH, D_MODEL), BF16),
            st_shape(n_a), st_shape(POOL_BUF), st_shape(n_c), st_shape(DEC_SEQ),
        ],
        input_output_aliases={N_SAMPLE_IN + k: k + 1 for k in range(len(in_place))},
        compiler_params=_params(("arbitrary",)),
        name=f"mixer_sample_l{layer}",
    )(p["sgu_w_small"], p["sgu_b_small"], proj3, sa, sp, sc,
      p["conv_a_w"], p["conv_a_b"], p["ln_a_g"], p["ln_a_b"], p["pool_w"], p["pool_scale"],
      p["sconv_w"], p["sgu_ln_g"], p["sgu_ln_b"], p["out_norm_g"], *in_place)
    return outs[0].reshape(M_S, D_MODEL), tuple(outs[1:])


FINAL_ROWS = 512


def _final_norm_kernel(x_ref, ss_ref, g_ref, o_ref):
    o_ref[...] = x_ref[...] * _row_scale(ss_ref) * g_ref[...]


def _final_norm(x, ss, g):
    def call(n_rows, first_block, name):
        rows = lambda c: pl.BlockSpec((FINAL_ROWS, c), lambda i: (first_block + i, 0))
        return pl.pallas_call(
            _final_norm_kernel,
            grid=(n_rows // FINAL_ROWS,),
            in_specs=[rows(D_MODEL), rows(1), pl.BlockSpec((1, D_MODEL), lambda i: (0, 0))],
            out_specs=pl.BlockSpec((FINAL_ROWS, D_MODEL), lambda i: (i, 0)),
            out_shape=jax.ShapeDtypeStruct((n_rows, D_MODEL), F32),
            compiler_params=_params(("arbitrary",)),
            name=name,
        )(x, ss, g)

    y_p = call(M_P, 0, "final_norm_prompt")
    y_s = call(M_S, M_P // FINAL_ROWS, "final_norm_sample")
    return (y_p.reshape(BATCH, SEQ, D_MODEL),
            jnp.swapaxes(y_s.reshape(DEC_SEQ, DEC_BATCH, D_MODEL), 0, 1))


def kernel(x_prompt, x_sample, state_conv_a, state_pool, state_sconv, norm_mix_g, w_in, conv_a_w, conv_a_b, ln_a_g, ln_a_b, pool_w, pool_scale, sconv_w, sgu_ln_g, sgu_ln_b, sgu_w, sgu_b, out_norm_g, w_out, norm_ffn_g, w_gate, w_up, w_down, final_norm_g):
    row = lambda a: a.reshape(DEPTH, 1, a.shape[-1])
    p = {
        "conv_a_w": conv_a_w, "conv_a_b": row(conv_a_b), "ln_a_g": row(ln_a_g), "ln_a_b": row(ln_a_b),
        "pool_w": pool_w, "pool_scale": row(pool_scale), "sconv_w": sconv_w,
        "sgu_ln_g": row(sgu_ln_g), "sgu_ln_b": row(sgu_ln_b), "sgu_w": sgu_w,
        "sgu_b_t": jnp.swapaxes(sgu_b, 1, 2),
        "sgu_w_small": sgu_w[:, :, :DEC_SEQ, :DEC_SEQ].reshape(-1),
        "sgu_b_small": sgu_b[:, :, :DEC_SEQ].reshape(-1),
        "out_norm_g": row(out_norm_g),
    }
    g_mix = row(norm_mix_g)
    g_ffn = row(norm_ffn_g)
    w_in_b = w_in[0].astype(BF16)

    rows_major = lambda a: jnp.swapaxes(a, 1, 2)
    states = (rows_major(state_conv_a), rows_major(state_pool), rows_major(state_sconv))

    x = (x_prompt.reshape(M_P, D_MODEL), jnp.swapaxes(x_sample, 0, 1).reshape(M_S, D_MODEL))
    xs, ss = _prep(*x, g_mix)

    ca_p, pl_p, sc_p = [], [], []
    stacked = ()
    for l in range(DEPTH):
        to_round = [(w_down, l), (w_out, l)] + ([(w_in, l + 1)] if l + 1 < DEPTH else [])
        mix_p, a_p, p_p, c_p, w_down_b, w_out_b, *w_in_next = _mixer_prompt(xs, ss, w_in_b, to_round, p, l)
        proj_s = _in_proj_sample(xs, ss, w_in_b, l)
        mix_s, stacked = _mixer_sample(proj_s, states, stacked, p, l)
        ca_p.append(a_p); pl_p.append(p_p); sc_p.append(c_p)
        x, xs, ss = _out_proj((mix_p, mix_s), w_out_b, x, g_ffn, l)
        h = _ffn_up(xs, ss, w_gate, w_up, l)
        x, xs, ss = _ffn_down(h, w_down_b, x, g_mix, l)
        w_in_b = w_in_next[0] if w_in_next else None

    y_prompt, y_sample = _final_norm(x, ss, final_norm_g.reshape(1, D_MODEL))
    ca_s, pl_s, sc_s, v_s = (rows_major(a) for a in stacked)
    return (y_prompt, y_sample, jnp.stack(ca_p), ca_s, jnp.stack(pl_p), pl_s, jnp.stack(sc_p), sc_s, v_s)
```

```python
import functools

import jax
import jax.numpy as jnp
from jax import lax
from jax.experimental import pallas as pl
from jax.experimental.pallas import tpu as pltpu

D_MODEL = 2048
BATCH = 4
SEQ = 2048
DEPTH = 4
DEC_BATCH = 128
DEC_SEQ = 4
PAST_LEN = 16384
D_GROUP = 512
N_SUB = 4
D_SUB = 128
N_PROJ = 8
D_PROJ = N_PROJ * D_GROUP
CONV_A_W = 31
POOL_WINDOWS = (2, 4, 8, 16)
POOL_BUF = 15
SCONV_W = 3
CHUNK = 128
D_FF = 5632
EPS = 1e-6

M_P = BATCH * SEQ
M_S = DEC_BATCH * DEC_SEQ
M_ALL = M_P + M_S

F32 = jnp.float32
BF16 = jnp.bfloat16

V7X_VMEM_LIMIT_BYTES = 56 * 1024 * 1024
V7X_VMEM_LIMIT_BIG_BYTES = 60 * 1024 * 1024

BM = 1088
BM_OUT = 512
BN_IN = 1024
BN_FF = 512
BN_DOWN = 512
CAST_ROWS = 256
PREP_ROWS = 512

BM_FF = 2176

TS = 512
RC = 128
HALO_A = 32
HALO_P = 16
HALO_C = 8

BB = 64


def _params(sem, vmem=V7X_VMEM_LIMIT_BYTES):
    return pltpu.CompilerParams(dimension_semantics=sem, vmem_limit_bytes=vmem)


def _cast_rows_to_bf16(w_ref, wb_ref):
    def body(i, carry):
        r = pl.multiple_of(i * CAST_ROWS, CAST_ROWS)
        wb_ref[pl.ds(r, CAST_ROWS), :] = w_ref[pl.ds(r, CAST_ROWS), :].astype(BF16)
        return carry

    lax.fori_loop(0, w_ref.shape[0] // CAST_ROWS, body, 0)


def _row_scale(ss_ref):
    return lax.rsqrt(ss_ref[...] * (1.0 / D_MODEL) + EPS)


def _sigmoid(x):
    return 0.5 * jnp.tanh(0.5 * x) + 0.5


def _two_source_specs(rows):
    n_p = M_P // rows
    return [pl.BlockSpec((rows, D_MODEL), lambda i: (jnp.minimum(i, n_p - 1), 0)),
            pl.BlockSpec((rows, D_MODEL), lambda i: (jnp.maximum(i - n_p, 0), 0))]


def _two_source_tile(xp_ref, xt_ref, rows):
    return jnp.where(pl.program_id(0) < M_P // rows, xp_ref[...], xt_ref[...])


def _prep_kernel(xp_ref, xt_ref, g_ref, xs_ref, ss_ref):
    x = _two_source_tile(xp_ref, xt_ref, PREP_ROWS)
    xs_ref[...] = (x * g_ref[...]).astype(BF16)
    ss_ref[...] = jnp.sum(x * x, axis=-1, keepdims=True)


def _prep(x_prompt, x_sample_tm, g):
    rows = pl.BlockSpec((PREP_ROWS, D_MODEL), lambda i: (i, 0))
    return pl.pallas_call(
        _prep_kernel,
        grid=(M_ALL // PREP_ROWS,),
        in_specs=_two_source_specs(PREP_ROWS) + [pl.BlockSpec((None, 1, D_MODEL), lambda i: (0, 0, 0))],
        out_specs=[rows, pl.BlockSpec((PREP_ROWS, 1), lambda i: (i, 0))],
        out_shape=[
            jax.ShapeDtypeStruct((M_ALL, D_MODEL), BF16),
            jax.ShapeDtypeStruct((M_ALL, 1), F32),
        ],
        compiler_params=_params(("arbitrary",)),
        name="prep",
    )(x_prompt, x_sample_tm, g)


def _in_proj_sample_kernel(xs_ref, ss_ref, w_ref, o_ref):
    o_ref[...] = jnp.dot(xs_ref[...], w_ref[...], preferred_element_type=F32) * _row_scale(ss_ref)


def _in_proj_sample(xs, ss, w, layer):
    first = M_P // M_S
    return pl.pallas_call(
        _in_proj_sample_kernel,
        grid=(D_PROJ // BN_IN,),
        in_specs=[
            pl.BlockSpec((M_S, D_MODEL), lambda j: (first, 0)),
            pl.BlockSpec((M_S, 1), lambda j: (first, 0)),
            pl.BlockSpec((D_MODEL, BN_IN), lambda j: (0, j)),
        ],
        out_specs=pl.BlockSpec((M_S, BN_IN), lambda j: (0, j)),
        out_shape=jax.ShapeDtypeStruct((M_S, D_PROJ), F32),
        compiler_params=_params(("arbitrary",)),
        name=f"in_proj_sample_l{layer}",
    )(xs, ss, w)


def _ffn_up_kernel(xs_ref, ss_ref, wg_ref, wu_ref, o_ref, wgb_ref, wub_ref):
    @pl.when(pl.program_id(1) == 0)
    def _():
        _cast_rows_to_bf16(wg_ref, wgb_ref)
        _cast_rows_to_bf16(wu_ref, wub_ref)

    scale = _row_scale(ss_ref)
    gate = jnp.dot(xs_ref[...], wgb_ref[...], preferred_element_type=F32) * scale
    up = jnp.dot(xs_ref[...], wub_ref[...], preferred_element_type=F32) * scale
    o_ref[...] = (gate * _sigmoid(gate) * up).astype(BF16)


def _ffn_up(xs, ss, wg, wu, layer):
    return pl.pallas_call(
        _ffn_up_kernel,
        grid=(D_FF // BN_FF, M_ALL // BM_FF),
        in_specs=[
            pl.BlockSpec((BM_FF, D_MODEL), lambda j, i: (i, 0)),
            pl.BlockSpec((BM_FF, 1), lambda j, i: (i, 0)),
            pl.BlockSpec((None, D_MODEL, BN_FF), lambda j, i: (layer, 0, j)),
            pl.BlockSpec((None, D_MODEL, BN_FF), lambda j, i: (layer, 0, j)),
        ],
        out_specs=pl.BlockSpec((BM_FF, BN_FF), lambda j, i: (i, j)),
        out_shape=jax.ShapeDtypeStruct((M_ALL, D_FF), BF16),
        scratch_shapes=[pltpu.VMEM((D_MODEL, BN_FF), BF16), pltpu.VMEM((D_MODEL, BN_FF), BF16)],
        compiler_params=_params(("arbitrary", "arbitrary"), V7X_VMEM_LIMIT_BIG_BYTES),
        name=f"ffn_up_l{layer}",
    )(xs, ss, wg, wu)


def _out_proj_kernel(ap_ref, at_ref, w_ref, g_ref, *refs):
    *res_refs, x_ref, xs_ref, ss_ref = refs
    a = _two_source_tile(ap_ref, at_ref, BM_OUT)
    res = res_refs[0][...] if len(res_refs) == 1 else _two_source_tile(*res_refs, BM_OUT)
    o = res + jnp.dot(a, w_ref[...], preferred_element_type=F32)
    x_ref[...] = o
    xs_ref[...] = (o * g_ref[...]).astype(BF16)
    ss_ref[...] = jnp.sum(o * o, axis=-1, keepdims=True)


def _out_proj(mix, w, res, g_next, layer):
    rows = pl.BlockSpec((BM_OUT, D_MODEL), lambda i: (i, 0))
    res = res if isinstance(res, tuple) else (res,)
    return pl.pallas_call(
        _out_proj_kernel,
        grid=(M_ALL // BM_OUT,),
        in_specs=_two_source_specs(BM_OUT) + [
            pl.BlockSpec((D_MODEL, D_MODEL), lambda i: (0, 0), pipeline_mode=pl.Buffered(1)),
            pl.BlockSpec((None, 1, D_MODEL), lambda i: (layer, 0, 0)),
        ] + ([rows] if len(res) == 1 else _two_source_specs(BM_OUT)),
        out_specs=[rows, rows, pl.BlockSpec((BM_OUT, 1), lambda i: (i, 0))],
        out_shape=[
            jax.ShapeDtypeStruct((M_ALL, D_MODEL), F32),
            jax.ShapeDtypeStruct((M_ALL, D_MODEL), BF16),
            jax.ShapeDtypeStruct((M_ALL, 1), F32),
        ],
        compiler_params=_params(("arbitrary",)),
        name=f"out_proj_l{layer}",
    )(*mix, w, g_next, *res)


def _ffn_down_kernel(*refs, has_next):
    if has_next:
        h_ref, w_ref, r_ref, g_ref, x_ref, xs_ref, ss_ref = refs
    else:
        h_ref, w_ref, r_ref, x_ref, ss_ref = refs
    o = r_ref[...] + jnp.dot(h_ref[...], w_ref[...], preferred_element_type=F32)
    x_ref[...] = o
    if has_next:
        xs_ref[...] = (o * g_ref[...]).astype(BF16)
    part = jnp.sum(o * o, axis=-1, keepdims=True)
    j = pl.program_id(1)

    @pl.when(j == 0)
    def _():
        ss_ref[...] = part

    @pl.when(j > 0)
    def _():
        ss_ref[...] += part


def _ffn_down(h, w, res, g_next, layer):
    has_next = layer + 1 < DEPTH
    tile = pl.BlockSpec((BM, BN_DOWN), lambda i, j: (i, j))
    ss_spec = pl.BlockSpec((BM, 1), lambda i, j: (i, 0))
    x_shape = jax.ShapeDtypeStruct((M_ALL, D_MODEL), F32)
    ss_shape = jax.ShapeDtypeStruct((M_ALL, 1), F32)
    in_specs = [pl.BlockSpec((BM, D_FF), lambda i, j: (i, 0)),
                pl.BlockSpec((D_FF, BN_DOWN), lambda i, j: (0, j)),
                tile]
    args = [h, w, res]
    out_specs, out_shape = [tile, ss_spec], [x_shape, ss_shape]
    if has_next:
        in_specs.append(pl.BlockSpec((None, 1, BN_DOWN), lambda i, j: (layer + 1, 0, j)))
        args.append(g_next)
        out_specs = [tile, tile, ss_spec]
        out_shape = [x_shape, jax.ShapeDtypeStruct((M_ALL, D_MODEL), BF16), ss_shape]
    outs = pl.pallas_call(
        functools.partial(_ffn_down_kernel, has_next=has_next),
        grid=(M_ALL // BM, D_MODEL // BN_DOWN),
        in_specs=in_specs,
        out_specs=out_specs,
        out_shape=out_shape,
        compiler_params=_params(("arbitrary", "arbitrary")),
        name=f"ffn_down_l{layer}",
    )(*args)
    return tuple(outs) if has_next else (outs[0], None, outs[1])


def _layernorm(x, g, b):
    mu = jnp.mean(x, axis=-1, keepdims=True)
    xc = x - mu
    var = jnp.mean(xc * xc, axis=-1, keepdims=True)
    return xc * lax.rsqrt(var + EPS) * g + b


def _rms(x, g):
    return x * lax.rsqrt(jnp.mean(x * x, axis=-1, keepdims=True) + EPS) * g


def _group(g):
    return slice(g * D_SUB, (g + 1) * D_SUB)


def _mixer(m):
    return slice(m * D_GROUP, (m + 1) * D_GROUP)


N_MIXER_IN = 15


def _mixer_prompt_kernel(*refs, n_cast):
    (xs_ref, ss_ref, w_ref, caw_ref, cab_ref, lag_ref, lab_ref, pw_ref, ps_ref, scw_ref,
     sg_ref, sb_ref, sw_ref, sbias_ref, ong_ref) = refs[:N_MIXER_IN]
    cast_in = refs[N_MIXER_IN:N_MIXER_IN + n_cast]
    mix_ref, sta_ref, stp_ref, stc_ref = refs[N_MIXER_IN + n_cast:N_MIXER_IN + n_cast + 4]
    cast_out = refs[N_MIXER_IN + n_cast + 4:N_MIXER_IN + 2 * n_cast + 4]
    gbuf, pbuf, cbuf, vbuf, *proj_refs = refs[N_MIXER_IN + 2 * n_cast + 4:]
    s = pl.program_id(1)
    streams = ((gbuf, HALO_A), (pbuf, HALO_P), (cbuf, HALO_C))

    @pl.when(s == 0)
    def _():
        for buf, halo in streams:
            buf[:, 0:halo, :] = jnp.zeros((N_SUB, halo, D_SUB), F32)

    @pl.when(s > 0)
    def _():
        for buf, halo in streams:
            buf[:, 0:halo, :] = buf[:, TS:TS + halo, :]

    scale = _row_scale(ss_ref)
    sg = sg_ref[...]
    sb = sb_ref[...]
    cab = cab_ref[...]
    lag = lag_ref[...]
    lab = lab_ref[...]
    ps = ps_ref[...]
    rows = lambda c: slice(c * RC, (c + 1) * RC)

    def project(k):
        proj_refs[k][...] = jnp.dot(xs_ref[...], w_ref[:, k * D_GROUP:(k + 1) * D_GROUP],
                                    preferred_element_type=F32) * scale

    def col(k, c, n=RC):
        return proj_refs[k][c * n:(c + 1) * n, :]

    def put(buf, halo, c, val):
        for j in range(N_SUB):
            buf[j, halo + c * RC:halo + (c + 1) * RC, :] = val[:, _group(j)]

    def window(buf, halo, j, c, back):
        lo = halo + c * RC - back
        return buf[j, lo:lo + RC, :]

    def stage_glu(c):
        put(gbuf, HALO_A, c, col(0, c) * _sigmoid(col(1, c)))

    def stage_pool(c):
        put(pbuf, HALO_P, c, col(2, c))

    def stage_gated(c):
        put(cbuf, HALO_C, c, col(5, c) * col(3, c))

    def stage_v(c):
        vbuf[rows(c), :] = _layernorm(col(7, c), sg, sb)

    def conv_module(c):
        tiles = []
        for j in range(N_SUB):
            acc = None
            for k in range(CONV_A_W):
                term = caw_ref[k:k + 1, _group(j)] * window(gbuf, HALO_A, j, c, CONV_A_W - 1 - k)
                acc = term if acc is None else acc + term
            tiles.append(acc)
        ya = _layernorm(jnp.concatenate(tiles, axis=-1) + cab, lag, lab)
        ya = ya * _sigmoid(ya)
        mix_ref[rows(c), _mixer(0)] = _rms(ya, ong_ref[:, _mixer(0)]).astype(BF16)

    def pooling(c):
        pos = s * TS + c * RC + lax.broadcasted_iota(jnp.int32, (RC, D_SUB), 0)
        yb = []
        for g, w in enumerate(POOL_WINDOWS):
            cur = window(pbuf, HALO_P, g, c, 0)
            tot = cur
            for d in range(1, w):
                tot = tot + window(pbuf, HALO_P, g, c, d)
            if c * RC + 1 >= w:
                pooled = tot / float(w) - cur
            else:
                pooled = tot / jnp.minimum(pos + 1, w).astype(F32) - cur
            yb.append(jnp.dot(pooled.astype(BF16), pw_ref[g].astype(BF16), preferred_element_type=F32))
        yb = jnp.concatenate(yb, axis=-1) * ps
        mix_ref[rows(c), _mixer(1)] = _rms(yb, ong_ref[:, _mixer(1)]).astype(BF16)

    def gated_conv(c):
        tiles = []
        for j in range(N_SUB):
            acc = None
            for k in range(SCONV_W):
                term = scw_ref[k:k + 1, _group(j)] * window(cbuf, HALO_C, j, c, SCONV_W - 1 - k)
                acc = term if acc is None else acc + term
            tiles.append(acc)
        yc = col(4, c) * jnp.concatenate(tiles, axis=-1)
        mix_ref[rows(c), _mixer(2)] = _rms(yc, ong_ref[:, _mixer(2)]).astype(BF16)

    row = lax.broadcasted_iota(jnp.int32, (CHUNK, CHUNK), 0)
    colid = lax.broadcasted_iota(jnp.int32, (CHUNK, CHUNK), 1)
    tril = (row >= colid).astype(F32)
    w_tril = [(sw_ref[g] * tril).astype(BF16) for g in range(N_SUB)]

    def spatial_gate(c):
        r0 = c * CHUNK
        v = vbuf[r0:r0 + CHUNK, :].astype(BF16)
        z = [jnp.dot(w_tril[g], v[:, _group(g)], preferred_element_type=F32) + sbias_ref[:, g:g + 1]
             for g in range(N_SUB)]
        yd = col(6, c, CHUNK) * jnp.concatenate(z, axis=-1)
        mix_ref[r0:r0 + CHUNK, _mixer(3)] = _rms(yd, ong_ref[:, _mixer(3)]).astype(BF16)

    every = lambda f: [functools.partial(f, c) for c in range(TS // RC)]
    convs = every(conv_module)
    gates = [functools.partial(spatial_gate, c) for c in range(TS // CHUNK)]
    plan = [
        (7, []),
        (0, every(stage_v)),
        (1, []),
        (6, every(stage_glu) + convs[0:1]),
        (5, convs[1:2] + gates[0:2]),
        (3, convs[2:3] + gates[2:4]),
        (4, convs[3:4] + every(stage_gated)),
        (2, convs[4:5] + every(gated_conv)),
        (None, convs[5:6] + every(stage_pool) + convs[6:8] + every(pooling)),
    ]
    for k, pieces in plan:
        if k is not None:
            project(k)
        for piece in pieces:
            piece()

    for src, dst in zip(cast_in, cast_out):
        dst[...] = src[...].astype(BF16)

    @pl.when(s == pl.num_programs(1) - 1)
    def _():
        def tail(buf, halo, n):
            return jnp.concatenate([buf[j, halo + TS - n:halo + TS, :] for j in range(N_SUB)], axis=-1)

        sta_ref[...] = tail(gbuf, HALO_A, CONV_A_W - 1)
        stp_ref[...] = tail(pbuf, HALO_P, POOL_BUF)
        stc_ref[...] = tail(cbuf, HALO_C, SCONV_W - 1)


def _mixer_prompt(xs, ss, w, to_round, p, layer):
    n_s = SEQ // TS
    steps = BATCH * n_s

    def lp(shape):
        nd = len(shape)
        return pl.BlockSpec((None,) + shape, lambda b, s: (layer,) + (0,) * nd)

    def cast_in(arr, l):
        return pl.BlockSpec((None, arr.shape[1] // steps, arr.shape[2]), lambda b, s: (l, b * n_s + s, 0))

    def cast_out(arr):
        return pl.BlockSpec((arr.shape[1] // steps, arr.shape[2]), lambda b, s: (b * n_s + s, 0))

    state = lambda n: pl.BlockSpec((None, n, D_GROUP), lambda b, s: (b, 0, 0))
    return pl.pallas_call(
        functools.partial(_mixer_prompt_kernel, n_cast=len(to_round)),
        grid=(BATCH, n_s),
        in_specs=[
            pl.BlockSpec((TS, D_MODEL), lambda b, s: (b * n_s + s, 0)),
            pl.BlockSpec((TS, 1), lambda b, s: (b * n_s + s, 0)),
            pl.BlockSpec((D_MODEL, D_PROJ), lambda b, s: (0, 0), pipeline_mode=pl.Buffered(1)),
            lp((CONV_A_W, D_GROUP)), lp((1, D_GROUP)), lp((1, D_GROUP)), lp((1, D_GROUP)),
            lp((N_SUB, D_SUB, D_SUB)), lp((1, D_GROUP)), lp((SCONV_W, D_GROUP)),
            lp((1, D_GROUP)), lp((1, D_GROUP)), lp((N_SUB, CHUNK, CHUNK)), lp((CHUNK, N_SUB)),
            lp((1, D_MODEL)),
        ] + [cast_in(arr, l) for arr, l in to_round],
        out_specs=[
            pl.BlockSpec((TS, D_MODEL), lambda b, s: (b * n_s + s, 0)),
            state(CONV_A_W - 1), state(POOL_BUF), state(SCONV_W - 1),
        ] + [cast_out(arr) for arr, _ in to_round],
        out_shape=[
            jax.ShapeDtypeStruct((M_P, D_MODEL), BF16),
            jax.ShapeDtypeStruct((BATCH, CONV_A_W - 1, D_GROUP), F32),
            jax.ShapeDtypeStruct((BATCH, POOL_BUF, D_GROUP), F32),
            jax.ShapeDtypeStruct((BATCH, SCONV_W - 1, D_GROUP), F32),
        ] + [jax.ShapeDtypeStruct(arr.shape[1:], BF16) for arr, _ in to_round],
        scratch_shapes=[
            pltpu.VMEM((N_SUB, HALO_A + TS, D_SUB), F32),
            pltpu.VMEM((N_SUB, HALO_P + TS, D_SUB), F32),
            pltpu.VMEM((N_SUB, HALO_C + TS, D_SUB), F32),
            pltpu.VMEM((TS, D_GROUP), F32),
        ] + [pltpu.VMEM((TS, D_GROUP), F32)] * N_PROJ,
        compiler_params=_params(("arbitrary", "arbitrary")),
        name=f"mixer_prompt_l{layer}",
    )(xs, ss, w, p["conv_a_w"], p["conv_a_b"], p["ln_a_g"], p["ln_a_b"], p["pool_w"], p["pool_scale"],
      p["sconv_w"], p["sgu_ln_g"], p["sgu_ln_b"], p["sgu_w"], p["sgu_b_t"], p["out_norm_g"],
      *[arr for arr, _ in to_round])


N_SAMPLE_IN = 16
N_SAMPLE_OUT = 5


def _mixer_sample_kernel(*refs, layer):
    (wsm_ref, bsm_ref, proj_ref, sa_ref, sp_ref, sc_ref, caw_ref, cab_ref, lag_ref, lab_ref, pw_ref,
     ps_ref, scw_ref, sg_ref, sb_ref, ong_ref) = refs[:N_SAMPLE_IN]
    mix_ref, nsa_ref, nsp_ref, nsc_ref, vn_ref = refs[-N_SAMPLE_OUT:]

    def col(t, k):
        return proj_ref[t, :, k * D_GROUP:(k + 1) * D_GROUP]

    def slab(j):
        return slice(j * D_GROUP, (j + 1) * D_GROUP)

    steps = range(DEC_SEQ)

    glu = [col(t, 0) * _sigmoid(col(t, 1)) for t in steps]
    n_a = CONV_A_W - 1

    old_a = [sa_ref[j] for j in range(n_a)]

    def za(j):
        return old_a[j] if j < n_a else glu[j - n_a]

    cab = cab_ref[...]
    lag = lag_ref[...]
    lab = lab_ref[...]
    for t in steps:
        acc = caw_ref[0:1, :] * za(t)
        for k in range(1, CONV_A_W):
            acc = acc + caw_ref[k:k + 1, :] * za(t + k)
        ya = _layernorm(acc + cab, lag, lab)
        ya = ya * _sigmoid(ya)
        mix_ref[t, :, _mixer(0)] = _rms(ya, ong_ref[:, _mixer(0)]).astype(BF16)
    for j in range(n_a):
        nsa_ref[j] = za(j + DEC_SEQ)

    pin = [col(t, 2) for t in steps]

    old_p = [sp_ref[j] for j in range(POOL_BUF)]

    def zp(j):
        return old_p[j] if j < POOL_BUF else pin[j - POOL_BUF]

    ps = ps_ref[...]
    for t in steps:
        yb = []
        for g, w in enumerate(POOL_WINDOWS):
            cnt = float(min(PAST_LEN + t + 1, w))
            tot = zp(POOL_BUF + t)[:, _group(g)]
            for d in range(1, w):
                tot = tot + zp(POOL_BUF + t - d)[:, _group(g)]
            pooled = tot / cnt - pin[t][:, _group(g)]
            yb.append(jnp.dot(pooled.astype(BF16), pw_ref[g].astype(BF16), preferred_element_type=F32))
        yb = jnp.concatenate(yb, axis=-1) * ps
        mix_ref[t, :, _mixer(1)] = _rms(yb, ong_ref[:, _mixer(1)]).astype(BF16)
    for j in range(POOL_BUF):
        nsp_ref[j] = zp(j + DEC_SEQ)

    gated = [col(t, 5) * col(t, 3) for t in steps]
    n_c = SCONV_W - 1

    old_c = [sc_ref[j] for j in range(n_c)]

    def zc(j):
        return old_c[j] if j < n_c else gated[j - n_c]

    for t in steps:
        conv = scw_ref[0:1, :] * zc(t) + scw_ref[1:2, :] * zc(t + 1) + scw_ref[2:3, :] * zc(t + 2)
        yc = col(t, 4) * conv
        mix_ref[t, :, _mixer(2)] = _rms(yc, ong_ref[:, _mixer(2)]).astype(BF16)
    for j in range(n_c):
        nsc_ref[j] = zc(j + DEC_SEQ)

    sg = sg_ref[...]
    sb = sb_ref[...]
    vn = [_layernorm(col(t, 7), sg, sb) for t in steps]
    for t in steps:
        vn_ref[t] = vn[t]
    for t in steps:
        z = []
        for g in range(N_SUB):
            base = (layer * N_SUB + g) * DEC_SEQ * DEC_SEQ + t * DEC_SEQ
            zg = wsm_ref[base] * vn[0][:, _group(g)]
            for j in range(1, t + 1):
                zg = zg + wsm_ref[base + j] * vn[j][:, _group(g)]
            z.append(zg + bsm_ref[(layer * N_SUB + g) * DEC_SEQ + t])
        yd = col(t, 6) * jnp.concatenate(z, axis=-1)
        mix_ref[t, :, _mixer(3)] = _rms(yd, ong_ref[:, _mixer(3)]).astype(BF16)


def _mixer_sample(proj, states, stacked, p, layer):
    sa, sp, sc = states
    proj3 = proj.reshape(DEC_SEQ, DEC_BATCH, D_PROJ)
    in_place = tuple(stacked)

    def lp(shape):
        nd = len(shape)
        return pl.BlockSpec((None,) + shape, lambda i: (layer,) + (0,) * nd)

    smem = pl.BlockSpec(memory_space=pltpu.SMEM)
    st = lambda n: pl.BlockSpec((None, n, BB, D_GROUP), lambda i: (layer, 0, i, 0))
    st_shape = lambda n: jax.ShapeDtypeStruct((DEPTH, n, DEC_BATCH, D_GROUP), F32)
    n_a, n_c = CONV_A_W - 1, SCONV_W - 1
    outs = pl.pallas_call(
        functools.partial(_mixer_sample_kernel, layer=layer),
        grid=(DEC_BATCH // BB,),
        in_specs=[
            smem, smem,
            pl.BlockSpec((DEC_SEQ, BB, D_PROJ), lambda i: (0, i, 0)),
            st(n_a), st(POOL_BUF), st(n_c),
            lp((CONV_A_W, D_GROUP)), lp((1, D_GROUP)), lp((1, D_GROUP)), lp((1, D_GROUP)),
            lp((N_SUB, D_SUB, D_SUB)), lp((1, D_GROUP)), lp((SCONV_W, D_GROUP)),
            lp((1, D_GROUP)), lp((1, D_GROUP)), lp((1, D_MODEL)),
        ] + [pl.BlockSpec(memory_space=pl.ANY)] * len(in_place),
        out_specs=[
            pl.BlockSpec((DEC_SEQ, BB, D_MODEL), lambda i: (0, i, 0)),
            st(n_a), st(POOL_BUF), st(n_c), st(DEC_SEQ),
        ],
        out_shape=[
            jax.ShapeDtypeStruct((DEC_SEQ, DEC_BATCH, D_MODEL), BF16),
            st_shape(n_a), st_shape(POOL_BUF), st_shape(n_c), st_shape(DEC_SEQ),
        ],
        input_output_aliases={N_SAMPLE_IN + k: k + 1 for k in range(len(in_place))},
        compiler_params=_params(("arbitrary",)),
        name=f"mixer_sample_l{layer}",
    )(p["sgu_w_small"], p["sgu_b_small"], proj3, sa, sp, sc,
      p["conv_a_w"], p["conv_a_b"], p["ln_a_g"], p["ln_a_b"], p["pool_w"], p["pool_scale"],
      p["sconv_w"], p["sgu_ln_g"], p["sgu_ln_b"], p["out_norm_g"], *in_place)
    return outs[0].reshape(M_S, D_MODEL), tuple(outs[1:])


FINAL_ROWS = 1024


def _final_norm_kernel(x_ref, ss_ref, g_ref, o_ref):
    o_ref[...] = x_ref[...] * _row_scale(ss_ref) * g_ref[...]


def _final_norm(x, ss, g):
    def call(n_rows, first_row, tile, name):
        rows = lambda c: pl.BlockSpec((tile, c), lambda i: (first_row // tile + i, 0))
        return pl.pallas_call(
            _final_norm_kernel,
            grid=(n_rows // tile,),
            in_specs=[rows(D_MODEL), rows(1), pl.BlockSpec((1, D_MODEL), lambda i: (0, 0))],
            out_specs=pl.BlockSpec((tile, D_MODEL), lambda i: (i, 0)),
            out_shape=jax.ShapeDtypeStruct((n_rows, D_MODEL), F32),
            compiler_params=_params(("arbitrary",)),
            name=name,
        )(x, ss, g)

    y_p = call(M_P, 0, FINAL_ROWS, "final_norm_prompt")
    y_s = call(M_S, M_P, M_S, "final_norm_sample")
    return (y_p.reshape(BATCH, SEQ, D_MODEL),
            jnp.swapaxes(y_s.reshape(DEC_SEQ, DEC_BATCH, D_MODEL), 0, 1))


def kernel(x_prompt, x_sample, state_conv_a, state_pool, state_sconv, norm_mix_g, w_in, conv_a_w, conv_a_b, ln_a_g, ln_a_b, pool_w, pool_scale, sconv_w, sgu_ln_g, sgu_ln_b, sgu_w, sgu_b, out_norm_g, w_out, norm_ffn_g, w_gate, w_up, w_down, final_norm_g):
    row = lambda a: a.reshape(DEPTH, 1, a.shape[-1])
    p = {
        "conv_a_w": conv_a_w, "conv_a_b": row(conv_a_b), "ln_a_g": row(ln_a_g), "ln_a_b": row(ln_a_b),
        "pool_w": pool_w, "pool_scale": row(pool_scale), "sconv_w": sconv_w,
        "sgu_ln_g": row(sgu_ln_g), "sgu_ln_b": row(sgu_ln_b), "sgu_w": sgu_w,
        "sgu_b_t": jnp.swapaxes(sgu_b, 1, 2),
        "sgu_w_small": sgu_w[:, :, :DEC_SEQ, :DEC_SEQ].reshape(-1),
        "sgu_b_small": sgu_b[:, :, :DEC_SEQ].reshape(-1),
        "out_norm_g": row(out_norm_g),
    }
    g_mix = row(norm_mix_g)
    g_ffn = row(norm_ffn_g)
    w_in_b = w_in[0].astype(BF16)

    rows_major = lambda a: jnp.swapaxes(a, 1, 2)
    states = (rows_major(state_conv_a), rows_major(state_pool), rows_major(state_sconv))

    x = (x_prompt.reshape(M_P, D_MODEL), jnp.swapaxes(x_sample, 0, 1).reshape(M_S, D_MODEL))
    xs, ss = _prep(*x, g_mix)

    ca_p, pl_p, sc_p = [], [], []
    stacked = ()
    for l in range(DEPTH):
        to_round = [(w_down, l), (w_out, l)] + ([(w_in, l + 1)] if l + 1 < DEPTH else [])
        mix_p, a_p, p_p, c_p, w_down_b, w_out_b, *w_in_next = _mixer_prompt(xs, ss, w_in_b, to_round, p, l)
        proj_s = _in_proj_sample(xs, ss, w_in_b, l)
        mix_s, stacked = _mixer_sample(proj_s, states, stacked, p, l)
        ca_p.append(a_p); pl_p.append(p_p); sc_p.append(c_p)
        x, xs, ss = _out_proj((mix_p, mix_s), w_out_b, x, g_ffn, l)
        h = _ffn_up(xs, ss, w_gate, w_up, l)
        x, xs, ss = _ffn_down(h, w_down_b, x, g_mix, l)
        w_in_b = w_in_next[0] if w_in_next else None

    y_prompt, y_sample = _final_norm(x, ss, final_norm_g.reshape(1, D_MODEL))
    ca_s, pl_s, sc_s, v_s = (rows_major(a) for a in stacked)
    return (y_prompt, y_sample, jnp.stack(ca_p), ca_s, jnp.stack(pl_p), pl_s, jnp.stack(sc_p), sc_s, v_s)
```

```python
import functools

import jax
import jax.numpy as jnp
from jax import lax
from jax.experimental import pallas as pl
from jax.experimental.pallas import tpu as pltpu

D_MODEL = 2048
BATCH = 4
SEQ = 2048
DEPTH = 4
DEC_BATCH = 128
DEC_SEQ = 4
PAST_LEN = 16384
D_GROUP = 512
N_SUB = 4
D_SUB = 128
N_PROJ = 8
D_PROJ = N_PROJ * D_GROUP
CONV_A_W = 31
POOL_WINDOWS = (2, 4, 8, 16)
POOL_BUF = 15
SCONV_W = 3
CHUNK = 128
D_FF = 5632
EPS = 1e-6

M_P = BATCH * SEQ
M_S = DEC_BATCH * DEC_SEQ
M_ALL = M_P + M_S

F32 = jnp.float32
BF16 = jnp.bfloat16

V7X_VMEM_LIMIT_BYTES = 56 * 1024 * 1024
V7X_VMEM_LIMIT_BIG_BYTES = 60 * 1024 * 1024

BM = 1088
BM_OUT = 512
BN_IN = 1024
BN_FF = 512
BN_DOWN = 512
CAST_ROWS = 256

BM_FF = 2176

TS = 512
RC = 128
HALO_A = 32
HALO_P = 16
HALO_C = 8

BB = 64


def _params(sem, vmem=V7X_VMEM_LIMIT_BYTES):
    return pltpu.CompilerParams(dimension_semantics=sem, vmem_limit_bytes=vmem)


def _cast_rows_to_bf16(w_ref, wb_ref):
    def body(i, carry):
        r = pl.multiple_of(i * CAST_ROWS, CAST_ROWS)
        wb_ref[pl.ds(r, CAST_ROWS), :] = w_ref[pl.ds(r, CAST_ROWS), :].astype(BF16)
        return carry

    lax.fori_loop(0, w_ref.shape[0] // CAST_ROWS, body, 0)


def _row_scale(ss_ref):
    return lax.rsqrt(ss_ref[...] * (1.0 / D_MODEL) + EPS)


def _sigmoid(x):
    return 0.5 * jnp.tanh(0.5 * x) + 0.5


def _two_source_specs(rows):
    n_p = M_P // rows
    return [pl.BlockSpec((rows, D_MODEL), lambda i: (jnp.minimum(i, n_p - 1), 0)),
            pl.BlockSpec((rows, D_MODEL), lambda i: (jnp.maximum(i - n_p, 0), 0))]


def _two_source_tile(xp_ref, xt_ref, rows):
    return jnp.where(pl.program_id(0) < M_P // rows, xp_ref[...], xt_ref[...])


def _in_proj_sample_kernel(xs_ref, ss_ref, w_ref, o_ref):
    o_ref[...] = jnp.dot(xs_ref[...], w_ref[...], preferred_element_type=F32) * _row_scale(ss_ref)


def _in_proj_sample_from_x_kernel(x_ref, g_ref, w_ref, o_ref):
    x = x_ref[...]
    scale = lax.rsqrt(jnp.mean(x * x, axis=-1, keepdims=True) + EPS)
    o_ref[...] = jnp.dot((x * g_ref[...]).astype(BF16), w_ref[...], preferred_element_type=F32) * scale


def _in_proj_sample(xs, ss, w, layer):
    if ss is None:
        body = _in_proj_sample_from_x_kernel
        operands = (xs[0], xs[1], w)
        lhs = [pl.BlockSpec((M_S, D_MODEL), lambda j: (0, 0)),
               pl.BlockSpec((None, 1, D_MODEL), lambda j: (layer, 0, 0))]
    else:
        first = M_P // M_S
        body = _in_proj_sample_kernel
        operands = (xs, ss, w)
        lhs = [pl.BlockSpec((M_S, D_MODEL), lambda j: (first, 0)),
               pl.BlockSpec((M_S, 1), lambda j: (first, 0))]
    return pl.pallas_call(
        body,
        grid=(D_PROJ // BN_IN,),
        in_specs=lhs + [pl.BlockSpec((D_MODEL, BN_IN), lambda j: (0, j))],
        out_specs=pl.BlockSpec((M_S, BN_IN), lambda j: (0, j)),
        out_shape=jax.ShapeDtypeStruct((M_S, D_PROJ), F32),
        compiler_params=_params(("arbitrary",)),
        name=f"in_proj_sample_l{layer}",
    )(*operands)


def _ffn_up_kernel(xs_ref, ss_ref, wg_ref, wu_ref, o_ref, wgb_ref, wub_ref):
    @pl.when(pl.program_id(1) == 0)
    def _():
        _cast_rows_to_bf16(wg_ref, wgb_ref)
        _cast_rows_to_bf16(wu_ref, wub_ref)

    scale = _row_scale(ss_ref)
    gate = jnp.dot(xs_ref[...], wgb_ref[...], preferred_element_type=F32) * scale
    up = jnp.dot(xs_ref[...], wub_ref[...], preferred_element_type=F32) * scale
    o_ref[...] = (gate * _sigmoid(gate) * up).astype(BF16)


def _ffn_up(xs, ss, wg, wu, layer):
    return pl.pallas_call(
        _ffn_up_kernel,
        grid=(D_FF // BN_FF, M_ALL // BM_FF),
        in_specs=[
            pl.BlockSpec((BM_FF, D_MODEL), lambda j, i: (i, 0)),
            pl.BlockSpec((BM_FF, 1), lambda j, i: (i, 0)),
            pl.BlockSpec((None, D_MODEL, BN_FF), lambda j, i: (layer, 0, j)),
            pl.BlockSpec((None, D_MODEL, BN_FF), lambda j, i: (layer, 0, j)),
        ],
        out_specs=pl.BlockSpec((BM_FF, BN_FF), lambda j, i: (i, j)),
        out_shape=jax.ShapeDtypeStruct((M_ALL, D_FF), BF16),
        scratch_shapes=[pltpu.VMEM((D_MODEL, BN_FF), BF16), pltpu.VMEM((D_MODEL, BN_FF), BF16)],
        compiler_params=_params(("arbitrary", "arbitrary"), V7X_VMEM_LIMIT_BIG_BYTES),
        name=f"ffn_up_l{layer}",
    )(xs, ss, wg, wu)


def _out_proj_kernel(ap_ref, at_ref, w_ref, g_ref, *refs):
    *res_refs, x_ref, xs_ref, ss_ref = refs
    a = _two_source_tile(ap_ref, at_ref, BM_OUT)
    res = res_refs[0][...] if len(res_refs) == 1 else _two_source_tile(*res_refs, BM_OUT)
    o = res + jnp.dot(a, w_ref[...], preferred_element_type=F32)
    x_ref[...] = o
    xs_ref[...] = (o * g_ref[...]).astype(BF16)
    ss_ref[...] = jnp.sum(o * o, axis=-1, keepdims=True)


def _out_proj(mix, w, res, g_next, layer):
    rows = pl.BlockSpec((BM_OUT, D_MODEL), lambda i: (i, 0))
    res = res if isinstance(res, tuple) else (res,)
    return pl.pallas_call(
        _out_proj_kernel,
        grid=(M_ALL // BM_OUT,),
        in_specs=_two_source_specs(BM_OUT) + [
            pl.BlockSpec((D_MODEL, D_MODEL), lambda i: (0, 0), pipeline_mode=pl.Buffered(1)),
            pl.BlockSpec((None, 1, D_MODEL), lambda i: (layer, 0, 0)),
        ] + ([rows] if len(res) == 1 else _two_source_specs(BM_OUT)),
        out_specs=[rows, rows, pl.BlockSpec((BM_OUT, 1), lambda i: (i, 0))],
        out_shape=[
            jax.ShapeDtypeStruct((M_ALL, D_MODEL), F32),
            jax.ShapeDtypeStruct((M_ALL, D_MODEL), BF16),
            jax.ShapeDtypeStruct((M_ALL, 1), F32),
        ],
        compiler_params=_params(("arbitrary",)),
        name=f"out_proj_l{layer}",
    )(*mix, w, g_next, *res)


def _ffn_down_kernel(*refs, has_next):
    if has_next:
        h_ref, w_ref, r_ref, g_ref, x_ref, xs_ref, ss_ref = refs
    else:
        h_ref, w_ref, r_ref, x_ref, ss_ref = refs
    o = r_ref[...] + jnp.dot(h_ref[...], w_ref[...], preferred_element_type=F32)
    x_ref[...] = o
    if has_next:
        xs_ref[...] = (o * g_ref[...]).astype(BF16)
    part = jnp.sum(o * o, axis=-1, keepdims=True)
    j = pl.program_id(1)

    @pl.when(j == 0)
    def _():
        ss_ref[...] = part

    @pl.when(j > 0)
    def _():
        ss_ref[...] += part


def _ffn_down(h, w, res, g_next, layer):
    has_next = layer + 1 < DEPTH
    tile = pl.BlockSpec((BM, BN_DOWN), lambda i, j: (i, j))
    ss_spec = pl.BlockSpec((BM, 1), lambda i, j: (i, 0))
    x_shape = jax.ShapeDtypeStruct((M_ALL, D_MODEL), F32)
    ss_shape = jax.ShapeDtypeStruct((M_ALL, 1), F32)
    in_specs = [pl.BlockSpec((BM, D_FF), lambda i, j: (i, 0)),
                pl.BlockSpec((D_FF, BN_DOWN), lambda i, j: (0, j)),
                tile]
    args = [h, w, res]
    out_specs, out_shape = [tile, ss_spec], [x_shape, ss_shape]
    if has_next:
        in_specs.append(pl.BlockSpec((None, 1, BN_DOWN), lambda i, j: (layer + 1, 0, j)))
        args.append(g_next)
        out_specs = [tile, tile, ss_spec]
        out_shape = [x_shape, jax.ShapeDtypeStruct((M_ALL, D_MODEL), BF16), ss_shape]
    outs = pl.pallas_call(
        functools.partial(_ffn_down_kernel, has_next=has_next),
        grid=(M_ALL // BM, D_MODEL // BN_DOWN),
        in_specs=in_specs,
        out_specs=out_specs,
        out_shape=out_shape,
        compiler_params=_params(("arbitrary", "arbitrary")),
        name=f"ffn_down_l{layer}",
    )(*args)
    return tuple(outs) if has_next else (outs[0], None, outs[1])


def _layernorm(x, g, b):
    mu = jnp.mean(x, axis=-1, keepdims=True)
    xc = x - mu
    var = jnp.mean(xc * xc, axis=-1, keepdims=True)
    return xc * lax.rsqrt(var + EPS) * g + b


def _rms(x, g):
    return x * lax.rsqrt(jnp.mean(x * x, axis=-1, keepdims=True) + EPS) * g


def _group(g):
    return slice(g * D_SUB, (g + 1) * D_SUB)


def _mixer(m):
    return slice(m * D_GROUP, (m + 1) * D_GROUP)


N_MIXER_IN = 15


def _mixer_prompt_kernel(*refs, n_cast, from_x):
    (xs_ref, ss_ref, w_ref, caw_ref, cab_ref, lag_ref, lab_ref, pw_ref, ps_ref, scw_ref,
     sg_ref, sb_ref, sw_ref, sbias_ref, ong_ref) = refs[:N_MIXER_IN]
    cast_in = refs[N_MIXER_IN:N_MIXER_IN + n_cast]
    mix_ref, sta_ref, stp_ref, stc_ref = refs[N_MIXER_IN + n_cast:N_MIXER_IN + n_cast + 4]
    cast_out = refs[N_MIXER_IN + n_cast + 4:N_MIXER_IN + 2 * n_cast + 4]
    gbuf, pbuf, cbuf, vbuf, *proj_refs = refs[N_MIXER_IN + 2 * n_cast + 4:]
    if from_x:
        x_ref, g_ref = xs_ref, ss_ref
        *proj_refs, xs_ref, ss_ref = proj_refs
    s = pl.program_id(1)
    streams = ((gbuf, HALO_A), (pbuf, HALO_P), (cbuf, HALO_C))

    @pl.when(s == 0)
    def _():
        for buf, halo in streams:
            buf[:, 0:halo, :] = jnp.zeros((N_SUB, halo, D_SUB), F32)

    @pl.when(s > 0)
    def _():
        for buf, halo in streams:
            buf[:, 0:halo, :] = buf[:, TS:TS + halo, :]

    if from_x:
        for c in range(TS // RC):
            x = x_ref[c * RC:(c + 1) * RC, :]
            xs_ref[c * RC:(c + 1) * RC, :] = (x * g_ref[...]).astype(BF16)
            ss_ref[c * RC:(c + 1) * RC, :] = jnp.sum(x * x, axis=-1, keepdims=True)
    scale = _row_scale(ss_ref)
    sg = sg_ref[...]
    sb = sb_ref[...]
    cab = cab_ref[...]
    lag = lag_ref[...]
    lab = lab_ref[...]
    ps = ps_ref[...]
    rows = lambda c: slice(c * RC, (c + 1) * RC)

    def project(k):
        proj_refs[k][...] = jnp.dot(xs_ref[...], w_ref[:, k * D_GROUP:(k + 1) * D_GROUP],
                                    preferred_element_type=F32) * scale

    def col(k, c, n=RC):
        return proj_refs[k][c * n:(c + 1) * n, :]

    def put(buf, halo, c, val):
        for j in range(N_SUB):
            buf[j, halo + c * RC:halo + (c + 1) * RC, :] = val[:, _group(j)]

    def window(buf, halo, j, c, back):
        lo = halo + c * RC - back
        return buf[j, lo:lo + RC, :]

    def stage_glu(c):
        put(gbuf, HALO_A, c, col(0, c) * _sigmoid(col(1, c)))

    def stage_pool(c):
        put(pbuf, HALO_P, c, col(2, c))

    def stage_gated(c):
        put(cbuf, HALO_C, c, col(5, c) * col(3, c))

    def stage_v(c):
        vbuf[rows(c), :] = _layernorm(col(7, c), sg, sb)

    def conv_module(c):
        tiles = []
        for j in range(N_SUB):
            acc = None
            for k in range(CONV_A_W):
                term = caw_ref[k:k + 1, _group(j)] * window(gbuf, HALO_A, j, c, CONV_A_W - 1 - k)
                acc = term if acc is None else acc + term
            tiles.append(acc)
        ya = _layernorm(jnp.concatenate(tiles, axis=-1) + cab, lag, lab)
        ya = ya * _sigmoid(ya)
        mix_ref[rows(c), _mixer(0)] = _rms(ya, ong_ref[:, _mixer(0)]).astype(BF16)

    def pooling(c):
        pos = s * TS + c * RC + lax.broadcasted_iota(jnp.int32, (RC, D_SUB), 0)
        yb = []
        for g, w in enumerate(POOL_WINDOWS):
            cur = window(pbuf, HALO_P, g, c, 0)
            tot = cur
            for d in range(1, w):
                tot = tot + window(pbuf, HALO_P, g, c, d)
            if c * RC + 1 >= w:
                pooled = tot / float(w) - cur
            else:
                pooled = tot / jnp.minimum(pos + 1, w).astype(F32) - cur
            yb.append(jnp.dot(pooled.astype(BF16), pw_ref[g].astype(BF16), preferred_element_type=F32))
        yb = jnp.concatenate(yb, axis=-1) * ps
        mix_ref[rows(c), _mixer(1)] = _rms(yb, ong_ref[:, _mixer(1)]).astype(BF16)

    def gated_conv(c):
        tiles = []
        for j in range(N_SUB):
            acc = None
            for k in range(SCONV_W):
                term = scw_ref[k:k + 1, _group(j)] * window(cbuf, HALO_C, j, c, SCONV_W - 1 - k)
                acc = term if acc is None else acc + term
            tiles.append(acc)
        yc = col(4, c) * jnp.concatenate(tiles, axis=-1)
        mix_ref[rows(c), _mixer(2)] = _rms(yc, ong_ref[:, _mixer(2)]).astype(BF16)

    row = lax.broadcasted_iota(jnp.int32, (CHUNK, CHUNK), 0)
    colid = lax.broadcasted_iota(jnp.int32, (CHUNK, CHUNK), 1)
    tril = (row >= colid).astype(F32)
    w_tril = [(sw_ref[g] * tril).astype(BF16) for g in range(N_SUB)]

    def spatial_gate(c):
        r0 = c * CHUNK
        v = vbuf[r0:r0 + CHUNK, :].astype(BF16)
        z = [jnp.dot(w_tril[g], v[:, _group(g)], preferred_element_type=F32) + sbias_ref[:, g:g + 1]
             for g in range(N_SUB)]
        yd = col(6, c, CHUNK) * jnp.concatenate(z, axis=-1)
        mix_ref[r0:r0 + CHUNK, _mixer(3)] = _rms(yd, ong_ref[:, _mixer(3)]).astype(BF16)

    every = lambda f: [functools.partial(f, c) for c in range(TS // RC)]
    convs = every(conv_module)
    gates = [functools.partial(spatial_gate, c) for c in range(TS // CHUNK)]
    plan = [
        (7, []),
        (0, every(stage_v)),
        (1, []),
        (6, every(stage_glu) + convs[0:1]),
        (5, convs[1:2] + gates[0:2]),
        (3, convs[2:3] + gates[2:4]),
        (4, convs[3:4] + every(stage_gated)),
        (2, convs[4:5] + every(gated_conv)),
        (None, convs[5:6] + every(stage_pool) + convs[6:8] + every(pooling)),
    ]
    for k, pieces in plan:
        if k is not None:
            project(k)
        for piece in pieces:
            piece()

    for src, dst in zip(cast_in, cast_out):
        dst[...] = src[...].astype(BF16)

    @pl.when(s == pl.num_programs(1) - 1)
    def _():
        def tail(buf, halo, n):
            return jnp.concatenate([buf[j, halo + TS - n:halo + TS, :] for j in range(N_SUB)], axis=-1)

        sta_ref[...] = tail(gbuf, HALO_A, CONV_A_W - 1)
        stp_ref[...] = tail(pbuf, HALO_P, POOL_BUF)
        stc_ref[...] = tail(cbuf, HALO_C, SCONV_W - 1)


def _mixer_prompt(xs, ss, w, to_round, p, layer):
    n_s = SEQ // TS
    steps = BATCH * n_s
    from_x = ss is None
    if from_x:
        xs, ss = xs
        lhs = [pl.BlockSpec((TS, D_MODEL), lambda b, s: (b * n_s + s, 0)),
               pl.BlockSpec((None, 1, D_MODEL), lambda b, s: (layer, 0, 0))]
        extra_scratch = [pltpu.VMEM((TS, D_MODEL), BF16), pltpu.VMEM((TS, 1), F32)]
    else:
        lhs = [pl.BlockSpec((TS, D_MODEL), lambda b, s: (b * n_s + s, 0)),
               pl.BlockSpec((TS, 1), lambda b, s: (b * n_s + s, 0))]
        extra_scratch = []

    def lp(shape):
        nd = len(shape)
        return pl.BlockSpec((None,) + shape, lambda b, s: (layer,) + (0,) * nd)

    def cast_in(arr, l):
        return pl.BlockSpec((None, arr.shape[1] // steps, arr.shape[2]), lambda b, s: (l, b * n_s + s, 0))

    def cast_out(arr):
        return pl.BlockSpec((arr.shape[1] // steps, arr.shape[2]), lambda b, s: (b * n_s + s, 0))

    state = lambda n: pl.BlockSpec((None, n, D_GROUP), lambda b, s: (b, 0, 0))
    return pl.pallas_call(
        functools.partial(_mixer_prompt_kernel, n_cast=len(to_round), from_x=from_x),
        grid=(BATCH, n_s),
        in_specs=lhs + [
            pl.BlockSpec((D_MODEL, D_PROJ), lambda b, s: (0, 0), pipeline_mode=pl.Buffered(1)),
            lp((CONV_A_W, D_GROUP)), lp((1, D_GROUP)), lp((1, D_GROUP)), lp((1, D_GROUP)),
            lp((N_SUB, D_SUB, D_SUB)), lp((1, D_GROUP)), lp((SCONV_W, D_GROUP)),
            lp((1, D_GROUP)), lp((1, D_GROUP)), lp((N_SUB, CHUNK, CHUNK)), lp((CHUNK, N_SUB)),
            lp((1, D_MODEL)),
        ] + [cast_in(arr, l) for arr, l in to_round],
        out_specs=[
            pl.BlockSpec((TS, D_MODEL), lambda b, s: (b * n_s + s, 0)),
            state(CONV_A_W - 1), state(POOL_BUF), state(SCONV_W - 1),
        ] + [cast_out(arr) for arr, _ in to_round],
        out_shape=[
            jax.ShapeDtypeStruct((M_P, D_MODEL), BF16),
            jax.ShapeDtypeStruct((BATCH, CONV_A_W - 1, D_GROUP), F32),
            jax.ShapeDtypeStruct((BATCH, POOL_BUF, D_GROUP), F32),
            jax.ShapeDtypeStruct((BATCH, SCONV_W - 1, D_GROUP), F32),
        ] + [jax.ShapeDtypeStruct(arr.shape[1:], BF16) for arr, _ in to_round],
        scratch_shapes=[
            pltpu.VMEM((N_SUB, HALO_A + TS, D_SUB), F32),
            pltpu.VMEM((N_SUB, HALO_P + TS, D_SUB), F32),
            pltpu.VMEM((N_SUB, HALO_C + TS, D_SUB), F32),
            pltpu.VMEM((TS, D_GROUP), F32),
        ] + [pltpu.VMEM((TS, D_GROUP), F32)] * N_PROJ + extra_scratch,
        compiler_params=_params(("arbitrary", "arbitrary")),
        name=f"mixer_prompt_l{layer}",
    )(xs, ss, w, p["conv_a_w"], p["conv_a_b"], p["ln_a_g"], p["ln_a_b"], p["pool_w"], p["pool_scale"],
      p["sconv_w"], p["sgu_ln_g"], p["sgu_ln_b"], p["sgu_w"], p["sgu_b_t"], p["out_norm_g"],
      *[arr for arr, _ in to_round])


N_SAMPLE_IN = 16
N_SAMPLE_OUT = 5


def _mixer_sample_kernel(*refs, layer):
    (wsm_ref, bsm_ref, proj_ref, sa_ref, sp_ref, sc_ref, caw_ref, cab_ref, lag_ref, lab_ref, pw_ref,
     ps_ref, scw_ref, sg_ref, sb_ref, ong_ref) = refs[:N_SAMPLE_IN]
    mix_ref, nsa_ref, nsp_ref, nsc_ref, vn_ref = refs[-N_SAMPLE_OUT:]

    def col(t, k):
        return proj_ref[t, :, k * D_GROUP:(k + 1) * D_GROUP]

    def slab(j):
        return slice(j * D_GROUP, (j + 1) * D_GROUP)

    steps = range(DEC_SEQ)

    glu = [col(t, 0) * _sigmoid(col(t, 1)) for t in steps]
    n_a = CONV_A_W - 1

    old_a = [sa_ref[j] for j in range(n_a)]

    def za(j):
        return old_a[j] if j < n_a else glu[j - n_a]

    cab = cab_ref[...]
    lag = lag_ref[...]
    lab = lab_ref[...]
    for t in steps:
        acc = caw_ref[0:1, :] * za(t)
        for k in range(1, CONV_A_W):
            acc = acc + caw_ref[k:k + 1, :] * za(t + k)
        ya = _layernorm(acc + cab, lag, lab)
        ya = ya * _sigmoid(ya)
        mix_ref[t, :, _mixer(0)] = _rms(ya, ong_ref[:, _mixer(0)]).astype(BF16)
    for j in range(n_a):
        nsa_ref[j] = za(j + DEC_SEQ)

    pin = [col(t, 2) for t in steps]

    old_p = [sp_ref[j] for j in range(POOL_BUF)]

    def zp(j):
        return old_p[j] if j < POOL_BUF else pin[j - POOL_BUF]

    ps = ps_ref[...]
    for t in steps:
        yb = []
        for g, w in enumerate(POOL_WINDOWS):
            cnt = float(min(PAST_LEN + t + 1, w))
            tot = zp(POOL_BUF + t)[:, _group(g)]
            for d in range(1, w):
                tot = tot + zp(POOL_BUF + t - d)[:, _group(g)]
            pooled = tot / cnt - pin[t][:, _group(g)]
            yb.append(jnp.dot(pooled.astype(BF16), pw_ref[g].astype(BF16), preferred_element_type=F32))
        yb = jnp.concatenate(yb, axis=-1) * ps
        mix_ref[t, :, _mixer(1)] = _rms(yb, ong_ref[:, _mixer(1)]).astype(BF16)
    for j in range(POOL_BUF):
        nsp_ref[j] = zp(j + DEC_SEQ)

    gated = [col(t, 5) * col(t, 3) for t in steps]
    n_c = SCONV_W - 1

    old_c = [sc_ref[j] for j in range(n_c)]

    def zc(j):
        return old_c[j] if j < n_c else gated[j - n_c]

    for t in steps:
        conv = scw_ref[0:1, :] * zc(t) + scw_ref[1:2, :] * zc(t + 1) + scw_ref[2:3, :] * zc(t + 2)
        yc = col(t, 4) * conv
        mix_ref[t, :, _mixer(2)] = _rms(yc, ong_ref[:, _mixer(2)]).astype(BF16)
    for j in range(n_c):
        nsc_ref[j] = zc(j + DEC_SEQ)

    sg = sg_ref[...]
    sb = sb_ref[...]
    vn = [_layernorm(col(t, 7), sg, sb) for t in steps]
    for t in steps:
        vn_ref[t] = vn[t]
    for t in steps:
        z = []
        for g in range(N_SUB):
            base = (layer * N_SUB + g) * DEC_SEQ * DEC_SEQ + t * DEC_SEQ
            zg = wsm_ref[base] * vn[0][:, _group(g)]
            for j in range(1, t + 1):
                zg = zg + wsm_ref[base + j] * vn[j][:, _group(g)]
            z.append(zg + bsm_ref[(layer * N_SUB + g) * DEC_SEQ + t])
        yd = col(t, 6) * jnp.concatenate(z, axis=-1)
        mix_ref[t, :, _mixer(3)] = _rms(yd, ong_ref[:, _mixer(3)]).astype(BF16)


def _mixer_sample(proj, states, stacked, p, layer):
    sa, sp, sc = states
    proj3 = proj.reshape(DEC_SEQ, DEC_BATCH, D_PROJ)
    in_place = tuple(stacked)

    def lp(shape):
        nd = len(shape)
        return pl.BlockSpec((None,) + shape, lambda i: (layer,) + (0,) * nd)

    smem = pl.BlockSpec(memory_space=pltpu.SMEM)
    st = lambda n: pl.BlockSpec((None, n, BB, D_GROUP), lambda i: (layer, 0, i, 0))
    st_shape = lambda n: jax.ShapeDtypeStruct((DEPTH, n, DEC_BATCH, D_GROUP), F32)
    n_a, n_c = CONV_A_W - 1, SCONV_W - 1
    outs = pl.pallas_call(
        functools.partial(_mixer_sample_kernel, layer=layer),
        grid=(DEC_BATCH // BB,),
        in_specs=[
            smem, smem,
            pl.BlockSpec((DEC_SEQ, BB, D_PROJ), lambda i: (0, i, 0)),
            st(n_a), st(POOL_BUF), st(n_c),
            lp((CONV_A_W, D_GROUP)), lp((1, D_GROUP)), lp((1, D_GROUP)), lp((1, D_GROUP)),
            lp((N_SUB, D_SUB, D_SUB)), lp((1, D_GROUP)), lp((SCONV_W, D_GROUP)),
            lp((1, D_GROUP)), lp((1, D_GROUP)), lp((1, D_MODEL)),
        ] + [pl.BlockSpec(memory_space=pl.ANY)] * len(in_place),
        out_specs=[
            pl.BlockSpec((DEC_SEQ, BB, D_MODEL), lambda i: (0, i, 0)),
            st(n_a), st(POOL_BUF), st(n_c), st(DEC_SEQ),
        ],
        out_shape=[
            jax.ShapeDtypeStruct((DEC_SEQ, DEC_BATCH, D_MODEL), BF16),
            st_shape(n_a), st_shape(POOL_BUF), st_shape(n_c), st_shape(DEC_SEQ),
        ],
        input_output_aliases={N_SAMPLE_IN + k: k + 1 for k in range(len(in_place))},
        compiler_params=_params(("arbitrary",)),
        name=f"mixer_sample_l{layer}",
    )(p["sgu_w_small"], p["sgu_b_small"], proj3, sa, sp, sc,
      p["conv_a_w"], p["conv_a_b"], p["ln_a_g"], p["ln_a_b"], p["pool_w"], p["pool_scale"],
      p["sconv_w"], p["sgu_ln_g"], p["sgu_ln_b"], p["out_norm_g"], *in_place)
    return outs[0].reshape(M_S, D_MODEL), tuple(outs[1:])


FINAL_ROWS = 1024


def _final_norm_kernel(x_ref, ss_ref, g_ref, o_ref):
    o_ref[...] = x_ref[...] * _row_scale(ss_ref) * g_ref[...]


def _final_norm(x, ss, g):
    def call(n_rows, first_row, tile, name):
        rows = lambda c: pl.BlockSpec((tile, c), lambda i: (first_row // tile + i, 0))
        return pl.pallas_call(
            _final_norm_kernel,
            grid=(n_rows // tile,),
            in_specs=[rows(D_MODEL), rows(1), pl.BlockSpec((1, D_MODEL), lambda i: (0, 0))],
            out_specs=pl.BlockSpec((tile, D_MODEL), lambda i: (i, 0)),
            out_shape=jax.ShapeDtypeStruct((n_rows, D_MODEL), F32),
            compiler_params=_params(("arbitrary",)),
            name=name,
        )(x, ss, g)

    y_p = call(M_P, 0, FINAL_ROWS, "final_norm_prompt")
    y_s = call(M_S, M_P, M_S, "final_norm_sample")
    return (y_p.reshape(BATCH, SEQ, D_MODEL),
            jnp.swapaxes(y_s.reshape(DEC_SEQ, DEC_BATCH, D_MODEL), 0, 1))


def kernel(x_prompt, x_sample, state_conv_a, state_pool, state_sconv, norm_mix_g, w_in, conv_a_w, conv_a_b, ln_a_g, ln_a_b, pool_w, pool_scale, sconv_w, sgu_ln_g, sgu_ln_b, sgu_w, sgu_b, out_norm_g, w_out, norm_ffn_g, w_gate, w_up, w_down, final_norm_g):
    row = lambda a: a.reshape(DEPTH, 1, a.shape[-1])
    p = {
        "conv_a_w": conv_a_w, "conv_a_b": row(conv_a_b), "ln_a_g": row(ln_a_g), "ln_a_b": row(ln_a_b),
        "pool_w": pool_w, "pool_scale": row(pool_scale), "sconv_w": sconv_w,
        "sgu_ln_g": row(sgu_ln_g), "sgu_ln_b": row(sgu_ln_b), "sgu_w": sgu_w,
        "sgu_b_t": jnp.swapaxes(sgu_b, 1, 2),
        "sgu_w_small": sgu_w[:, :, :DEC_SEQ, :DEC_SEQ].reshape(-1),
        "sgu_b_small": sgu_b[:, :, :DEC_SEQ].reshape(-1),
        "out_norm_g": row(out_norm_g),
    }
    g_mix = row(norm_mix_g)
    g_ffn = row(norm_ffn_g)
    w_in_b = w_in[0].astype(BF16)

    rows_major = lambda a: jnp.swapaxes(a, 1, 2)
    states = (rows_major(state_conv_a), rows_major(state_pool), rows_major(state_sconv))

    x = (x_prompt.reshape(M_P, D_MODEL), jnp.swapaxes(x_sample, 0, 1).reshape(M_S, D_MODEL))
    xs, ss = None, None

    ca_p, pl_p, sc_p = [], [], []
    stacked = ()
    for l in range(DEPTH):
        to_round = [(w_down, l), (w_out, l)] + ([(w_in, l + 1)] if l + 1 < DEPTH else [])
        lhs_p, lhs_s = ((x[0], g_mix), (x[1], g_mix)) if l == 0 else (xs, xs)
        mix_p, a_p, p_p, c_p, w_down_b, w_out_b, *w_in_next = _mixer_prompt(lhs_p, ss, w_in_b, to_round, p, l)
        proj_s = _in_proj_sample(lhs_s, ss, w_in_b, l)
        mix_s, stacked = _mixer_sample(proj_s, states, stacked, p, l)
        ca_p.append(a_p); pl_p.append(p_p); sc_p.append(c_p)
        x, xs, ss = _out_proj((mix_p, mix_s), w_out_b, x, g_ffn, l)
        h = _ffn_up(xs, ss, w_gate, w_up, l)
        x, xs, ss = _ffn_down(h, w_down_b, x, g_mix, l)
        w_in_b = w_in_next[0] if w_in_next else None

    y_prompt, y_sample = _final_norm(x, ss, final_norm_g.reshape(1, D_MODEL))
    ca_s, pl_s, sc_s, v_s = (rows_major(a) for a in stacked)
    return (y_prompt, y_sample, jnp.stack(ca_p), ca_s, jnp.stack(pl_p), pl_s, jnp.stack(sc_p), sc_s, v_s)
```

```python
import functools

import jax
import jax.numpy as jnp
from jax import lax
from jax.experimental import pallas as pl
from jax.experimental.pallas import tpu as pltpu

D_MODEL = 2048
BATCH = 4
SEQ = 2048
DEPTH = 4
DEC_BATCH = 128
DEC_SEQ = 4
PAST_LEN = 16384
D_GROUP = 512
N_SUB = 4
D_SUB = 128
N_PROJ = 8
D_PROJ = N_PROJ * D_GROUP
CONV_A_W = 31
POOL_WINDOWS = (2, 4, 8, 16)
POOL_BUF = 15
SCONV_W = 3
CHUNK = 128
D_FF = 5632
EPS = 1e-6

M_P = BATCH * SEQ
M_S = DEC_BATCH * DEC_SEQ
M_ALL = M_P + M_S

F32 = jnp.float32
BF16 = jnp.bfloat16

V7X_VMEM_LIMIT_BYTES = 56 * 1024 * 1024
V7X_VMEM_LIMIT_BIG_BYTES = 60 * 1024 * 1024

BM = 1088
BM_OUT = 512
BN_IN = 2048
BN_FF = 512
BN_DOWN = 512
CAST_ROWS = 256

BM_FF = 2176

TS = 512
RC = 128
HALO_A = 32
HALO_P = 16
HALO_C = 8

BB = 64


def _params(sem, vmem=V7X_VMEM_LIMIT_BYTES):
    return pltpu.CompilerParams(dimension_semantics=sem, vmem_limit_bytes=vmem)


def _cast_rows_to_bf16(w_ref, wb_ref):
    def body(i, carry):
        r = pl.multiple_of(i * CAST_ROWS, CAST_ROWS)
        wb_ref[pl.ds(r, CAST_ROWS), :] = w_ref[pl.ds(r, CAST_ROWS), :].astype(BF16)
        return carry

    lax.fori_loop(0, w_ref.shape[0] // CAST_ROWS, body, 0)


def _row_scale(ss_ref):
    return lax.rsqrt(ss_ref[...] * (1.0 / D_MODEL) + EPS)


def _sigmoid(x):
    return 0.5 * jnp.tanh(0.5 * x) + 0.5


def _two_source_specs(rows):
    n_p = M_P // rows
    return [pl.BlockSpec((rows, D_MODEL), lambda i: (jnp.minimum(i, n_p - 1), 0)),
            pl.BlockSpec((rows, D_MODEL), lambda i: (jnp.maximum(i - n_p, 0), 0))]


def _two_source_tile(xp_ref, xt_ref, rows):
    return jnp.where(pl.program_id(0) < M_P // rows, xp_ref[...], xt_ref[...])


def _in_proj_sample_kernel(xs_ref, ss_ref, w_ref, o_ref):
    o_ref[...] = jnp.dot(xs_ref[...], w_ref[...], preferred_element_type=F32) * _row_scale(ss_ref)


def _in_proj_sample_from_x_kernel(x_ref, g_ref, w_ref, o_ref):
    x = x_ref[...]
    scale = lax.rsqrt(jnp.mean(x * x, axis=-1, keepdims=True) + EPS)
    o_ref[...] = jnp.dot((x * g_ref[...]).astype(BF16), w_ref[...], preferred_element_type=F32) * scale


def _in_proj_sample(xs, ss, w, layer):
    if ss is None:
        body = _in_proj_sample_from_x_kernel
        operands = (xs[0], xs[1], w)
        lhs = [pl.BlockSpec((M_S, D_MODEL), lambda j: (0, 0)),
               pl.BlockSpec((None, 1, D_MODEL), lambda j: (layer, 0, 0))]
    else:
        first = M_P // M_S
        body = _in_proj_sample_kernel
        operands = (xs, ss, w)
        lhs = [pl.BlockSpec((M_S, D_MODEL), lambda j: (first, 0)),
               pl.BlockSpec((M_S, 1), lambda j: (first, 0))]
    return pl.pallas_call(
        body,
        grid=(D_PROJ // BN_IN,),
        in_specs=lhs + [pl.BlockSpec((D_MODEL, BN_IN), lambda j: (0, j))],
        out_specs=pl.BlockSpec((M_S, BN_IN), lambda j: (0, j)),
        out_shape=jax.ShapeDtypeStruct((M_S, D_PROJ), F32),
        compiler_params=_params(("arbitrary",)),
        name=f"in_proj_sample_l{layer}",
    )(*operands)


def _ffn_up_kernel(xs_ref, ss_ref, wg_ref, wu_ref, o_ref, wgb_ref, wub_ref):
    @pl.when(pl.program_id(1) == 0)
    def _():
        _cast_rows_to_bf16(wg_ref, wgb_ref)
        _cast_rows_to_bf16(wu_ref, wub_ref)

    scale = _row_scale(ss_ref)
    gate = jnp.dot(xs_ref[...], wgb_ref[...], preferred_element_type=F32) * scale
    up = jnp.dot(xs_ref[...], wub_ref[...], preferred_element_type=F32) * scale
    o_ref[...] = (gate * _sigmoid(gate) * up).astype(BF16)


def _ffn_up(xs, ss, wg, wu, layer):
    return pl.pallas_call(
        _ffn_up_kernel,
        grid=(D_FF // BN_FF, M_ALL // BM_FF),
        in_specs=[
            pl.BlockSpec((BM_FF, D_MODEL), lambda j, i: (i, 0)),
            pl.BlockSpec((BM_FF, 1), lambda j, i: (i, 0)),
            pl.BlockSpec((None, D_MODEL, BN_FF), lambda j, i: (layer, 0, j)),
            pl.BlockSpec((None, D_MODEL, BN_FF), lambda j, i: (layer, 0, j)),
        ],
        out_specs=pl.BlockSpec((BM_FF, BN_FF), lambda j, i: (i, j)),
        out_shape=jax.ShapeDtypeStruct((M_ALL, D_FF), BF16),
        scratch_shapes=[pltpu.VMEM((D_MODEL, BN_FF), BF16), pltpu.VMEM((D_MODEL, BN_FF), BF16)],
        compiler_params=_params(("arbitrary", "arbitrary"), V7X_VMEM_LIMIT_BIG_BYTES),
        name=f"ffn_up_l{layer}",
    )(xs, ss, wg, wu)


def _out_proj_kernel(ap_ref, at_ref, w_ref, g_ref, *refs):
    *res_refs, x_ref, xs_ref, ss_ref = refs
    a = _two_source_tile(ap_ref, at_ref, BM_OUT)
    res = res_refs[0][...] if len(res_refs) == 1 else _two_source_tile(*res_refs, BM_OUT)
    o = res + jnp.dot(a, w_ref[...], preferred_element_type=F32)
    x_ref[...] = o
    xs_ref[...] = (o * g_ref[...]).astype(BF16)
    ss_ref[...] = jnp.sum(o * o, axis=-1, keepdims=True)


def _out_proj(mix, w, res, g_next, layer):
    rows = pl.BlockSpec((BM_OUT, D_MODEL), lambda i: (i, 0))
    res = res if isinstance(res, tuple) else (res,)
    return pl.pallas_call(
        _out_proj_kernel,
        grid=(M_ALL // BM_OUT,),
        in_specs=_two_source_specs(BM_OUT) + [
            pl.BlockSpec((D_MODEL, D_MODEL), lambda i: (0, 0), pipeline_mode=pl.Buffered(1)),
            pl.BlockSpec((None, 1, D_MODEL), lambda i: (layer, 0, 0)),
        ] + ([rows] if len(res) == 1 else _two_source_specs(BM_OUT)),
        out_specs=[rows, rows, pl.BlockSpec((BM_OUT, 1), lambda i: (i, 0))],
        out_shape=[
            jax.ShapeDtypeStruct((M_ALL, D_MODEL), F32),
            jax.ShapeDtypeStruct((M_ALL, D_MODEL), BF16),
            jax.ShapeDtypeStruct((M_ALL, 1), F32),
        ],
        compiler_params=_params(("arbitrary",)),
        name=f"out_proj_l{layer}",
    )(*mix, w, g_next, *res)


def _ffn_down_kernel(*refs, has_next):
    if has_next:
        h_ref, w_ref, r_ref, g_ref, x_ref, xs_ref, ss_ref = refs
    else:
        h_ref, w_ref, r_ref, x_ref, ss_ref = refs
    o = r_ref[...] + jnp.dot(h_ref[...], w_ref[...], preferred_element_type=F32)
    x_ref[...] = o
    if has_next:
        xs_ref[...] = (o * g_ref[...]).astype(BF16)
    part = jnp.sum(o * o, axis=-1, keepdims=True)
    j = pl.program_id(1)

    @pl.when(j == 0)
    def _():
        ss_ref[...] = part

    @pl.when(j > 0)
    def _():
        ss_ref[...] += part


def _ffn_down(h, w, res, g_next, layer):
    has_next = layer + 1 < DEPTH
    tile = pl.BlockSpec((BM, BN_DOWN), lambda i, j: (i, j))
    ss_spec = pl.BlockSpec((BM, 1), lambda i, j: (i, 0))
    x_shape = jax.ShapeDtypeStruct((M_ALL, D_MODEL), F32)
    ss_shape = jax.ShapeDtypeStruct((M_ALL, 1), F32)
    in_specs = [pl.BlockSpec((BM, D_FF), lambda i, j: (i, 0)),
                pl.BlockSpec((D_FF, BN_DOWN), lambda i, j: (0, j)),
                tile]
    args = [h, w, res]
    out_specs, out_shape = [tile, ss_spec], [x_shape, ss_shape]
    if has_next:
        in_specs.append(pl.BlockSpec((None, 1, BN_DOWN), lambda i, j: (layer + 1, 0, j)))
        args.append(g_next)
        out_specs = [tile, tile, ss_spec]
        out_shape = [x_shape, jax.ShapeDtypeStruct((M_ALL, D_MODEL), BF16), ss_shape]
    outs = pl.pallas_call(
        functools.partial(_ffn_down_kernel, has_next=has_next),
        grid=(M_ALL // BM, D_MODEL // BN_DOWN),
        in_specs=in_specs,
        out_specs=out_specs,
        out_shape=out_shape,
        compiler_params=_params(("arbitrary", "arbitrary")),
        name=f"ffn_down_l{layer}",
    )(*args)
    return tuple(outs) if has_next else (outs[0], None, outs[1])


def _layernorm(x, g, b):
    mu = jnp.mean(x, axis=-1, keepdims=True)
    xc = x - mu
    var = jnp.mean(xc * xc, axis=-1, keepdims=True)
    return xc * lax.rsqrt(var + EPS) * g + b


def _rms(x, g):
    return x * lax.rsqrt(jnp.mean(x * x, axis=-1, keepdims=True) + EPS) * g


def _group(g):
    return slice(g * D_SUB, (g + 1) * D_SUB)


def _mixer(m):
    return slice(m * D_GROUP, (m + 1) * D_GROUP)


N_MIXER_IN = 15


def _mixer_prompt_kernel(*refs, n_cast, from_x):
    (xs_ref, ss_ref, w_ref, caw_ref, cab_ref, lag_ref, lab_ref, pw_ref, ps_ref, scw_ref,
     sg_ref, sb_ref, sw_ref, sbias_ref, ong_ref) = refs[:N_MIXER_IN]
    cast_in = refs[N_MIXER_IN:N_MIXER_IN + n_cast]
    mix_ref, sta_ref, stp_ref, stc_ref = refs[N_MIXER_IN + n_cast:N_MIXER_IN + n_cast + 4]
    cast_out = refs[N_MIXER_IN + n_cast + 4:N_MIXER_IN + 2 * n_cast + 4]
    gbuf, pbuf, cbuf, vbuf, *proj_refs = refs[N_MIXER_IN + 2 * n_cast + 4:]
    if from_x:
        x_ref, g_ref = xs_ref, ss_ref
        *proj_refs, xs_ref, ss_ref = proj_refs
    s = pl.program_id(1)
    streams = ((gbuf, HALO_A), (pbuf, HALO_P), (cbuf, HALO_C))

    @pl.when(s == 0)
    def _():
        for buf, halo in streams:
            buf[:, 0:halo, :] = jnp.zeros((N_SUB, halo, D_SUB), F32)

    @pl.when(s > 0)
    def _():
        for buf, halo in streams:
            buf[:, 0:halo, :] = buf[:, TS:TS + halo, :]

    if from_x:
        for c in range(TS // RC):
            x = x_ref[c * RC:(c + 1) * RC, :]
            xs_ref[c * RC:(c + 1) * RC, :] = (x * g_ref[...]).astype(BF16)
            ss_ref[c * RC:(c + 1) * RC, :] = jnp.sum(x * x, axis=-1, keepdims=True)
    scale = _row_scale(ss_ref)
    sg = sg_ref[...]
    sb = sb_ref[...]
    cab = cab_ref[...]
    lag = lag_ref[...]
    lab = lab_ref[...]
    ps = ps_ref[...]
    rows = lambda c: slice(c * RC, (c + 1) * RC)

    def project(k):
        proj_refs[k][...] = jnp.dot(xs_ref[...], w_ref[:, k * D_GROUP:(k + 1) * D_GROUP],
                                    preferred_element_type=F32) * scale

    def col(k, c, n=RC):
        return proj_refs[k][c * n:(c + 1) * n, :]

    def put(buf, halo, c, val):
        for j in range(N_SUB):
            buf[j, halo + c * RC:halo + (c + 1) * RC, :] = val[:, _group(j)]

    def window(buf, halo, j, c, back):
        lo = halo + c * RC - back
        return buf[j, lo:lo + RC, :]

    def stage_glu(c):
        put(gbuf, HALO_A, c, col(0, c) * _sigmoid(col(1, c)))

    def stage_pool(c):
        put(pbuf, HALO_P, c, col(2, c))

    def stage_gated(c):
        put(cbuf, HALO_C, c, col(5, c) * col(3, c))

    def stage_v(c):
        vbuf[rows(c), :] = _layernorm(col(7, c), sg, sb)

    def conv_module(c):
        tiles = []
        for j in range(N_SUB):
            acc = None
            for k in range(CONV_A_W):
                term = caw_ref[k:k + 1, _group(j)] * window(gbuf, HALO_A, j, c, CONV_A_W - 1 - k)
                acc = term if acc is None else acc + term
            tiles.append(acc)
        ya = _layernorm(jnp.concatenate(tiles, axis=-1) + cab, lag, lab)
        ya = ya * _sigmoid(ya)
        mix_ref[rows(c), _mixer(0)] = _rms(ya, ong_ref[:, _mixer(0)]).astype(BF16)

    def pooling(c):
        pos = s * TS + c * RC + lax.broadcasted_iota(jnp.int32, (RC, D_SUB), 0)
        yb = []
        for g, w in enumerate(POOL_WINDOWS):
            cur = window(pbuf, HALO_P, g, c, 0)
            tot = cur
            for d in range(1, w):
                tot = tot + window(pbuf, HALO_P, g, c, d)
            if c * RC + 1 >= w:
                pooled = tot / float(w) - cur
            else:
                pooled = tot / jnp.minimum(pos + 1, w).astype(F32) - cur
            yb.append(jnp.dot(pooled.astype(BF16), pw_ref[g].astype(BF16), preferred_element_type=F32))
        yb = jnp.concatenate(yb, axis=-1) * ps
        mix_ref[rows(c), _mixer(1)] = _rms(yb, ong_ref[:, _mixer(1)]).astype(BF16)

    def gated_conv(c):
        tiles = []
        for j in range(N_SUB):
            acc = None
            for k in range(SCONV_W):
                term = scw_ref[k:k + 1, _group(j)] * window(cbuf, HALO_C, j, c, SCONV_W - 1 - k)
                acc = term if acc is None else acc + term
            tiles.append(acc)
        yc = col(4, c) * jnp.concatenate(tiles, axis=-1)
        mix_ref[rows(c), _mixer(2)] = _rms(yc, ong_ref[:, _mixer(2)]).astype(BF16)

    row = lax.broadcasted_iota(jnp.int32, (CHUNK, CHUNK), 0)
    colid = lax.broadcasted_iota(jnp.int32, (CHUNK, CHUNK), 1)
    tril = (row >= colid).astype(F32)
    w_tril = [(sw_ref[g] * tril).astype(BF16) for g in range(N_SUB)]

    def spatial_gate(c):
        r0 = c * CHUNK
        v = vbuf[r0:r0 + CHUNK, :].astype(BF16)
        z = [jnp.dot(w_tril[g], v[:, _group(g)], preferred_element_type=F32) + sbias_ref[:, g:g + 1]
             for g in range(N_SUB)]
        yd = col(6, c, CHUNK) * jnp.concatenate(z, axis=-1)
        mix_ref[r0:r0 + CHUNK, _mixer(3)] = _rms(yd, ong_ref[:, _mixer(3)]).astype(BF16)

    every = lambda f: [functools.partial(f, c) for c in range(TS // RC)]
    convs = every(conv_module)
    gates = [functools.partial(spatial_gate, c) for c in range(TS // CHUNK)]
    plan = [
        (7, []),
        (0, every(stage_v)),
        (1, []),
        (6, every(stage_glu) + convs[0:1]),
        (5, convs[1:2] + gates[0:2]),
        (3, convs[2:3] + gates[2:4]),
        (4, convs[3:4] + every(stage_gated)),
        (2, convs[4:5] + every(gated_conv)),
        (None, convs[5:6] + every(stage_pool) + convs[6:8] + every(pooling)),
    ]
    for k, pieces in plan:
        if k is not None:
            project(k)
        for piece in pieces:
            piece()

    for src, dst in zip(cast_in, cast_out):
        dst[...] = src[...].astype(BF16)

    @pl.when(s == pl.num_programs(1) - 1)
    def _():
        def tail(buf, halo, n):
            return jnp.concatenate([buf[j, halo + TS - n:halo + TS, :] for j in range(N_SUB)], axis=-1)

        sta_ref[...] = tail(gbuf, HALO_A, CONV_A_W - 1)
        stp_ref[...] = tail(pbuf, HALO_P, POOL_BUF)
        stc_ref[...] = tail(cbuf, HALO_C, SCONV_W - 1)


def _mixer_prompt(xs, ss, w, to_round, p, layer):
    n_s = SEQ // TS
    steps = BATCH * n_s
    from_x = ss is None
    if from_x:
        xs, ss = xs
        lhs = [pl.BlockSpec((TS, D_MODEL), lambda b, s: (b * n_s + s, 0)),
               pl.BlockSpec((None, 1, D_MODEL), lambda b, s: (layer, 0, 0))]
        extra_scratch = [pltpu.VMEM((TS, D_MODEL), BF16), pltpu.VMEM((TS, 1), F32)]
    else:
        lhs = [pl.BlockSpec((TS, D_MODEL), lambda b, s: (b * n_s + s, 0)),
               pl.BlockSpec((TS, 1), lambda b, s: (b * n_s + s, 0))]
        extra_scratch = []

    def lp(shape):
        nd = len(shape)
        return pl.BlockSpec((None,) + shape, lambda b, s: (layer,) + (0,) * nd)

    def cast_in(arr, l):
        return pl.BlockSpec((None, arr.shape[1] // steps, arr.shape[2]), lambda b, s: (l, b * n_s + s, 0))

    def cast_out(arr):
        return pl.BlockSpec((arr.shape[1] // steps, arr.shape[2]), lambda b, s: (b * n_s + s, 0))

    state = lambda n: pl.BlockSpec((None, n, D_GROUP), lambda b, s: (b, 0, 0))
    return pl.pallas_call(
        functools.partial(_mixer_prompt_kernel, n_cast=len(to_round), from_x=from_x),
        grid=(BATCH, n_s),
        in_specs=lhs + [
            pl.BlockSpec((D_MODEL, D_PROJ), lambda b, s: (0, 0), pipeline_mode=pl.Buffered(1)),
            lp((CONV_A_W, D_GROUP)), lp((1, D_GROUP)), lp((1, D_GROUP)), lp((1, D_GROUP)),
            lp((N_SUB, D_SUB, D_SUB)), lp((1, D_GROUP)), lp((SCONV_W, D_GROUP)),
            lp((1, D_GROUP)), lp((1, D_GROUP)), lp((N_SUB, CHUNK, CHUNK)), lp((CHUNK, N_SUB)),
            lp((1, D_MODEL)),
        ] + [cast_in(arr, l) for arr, l in to_round],
        out_specs=[
            pl.BlockSpec((TS, D_MODEL), lambda b, s: (b * n_s + s, 0)),
            state(CONV_A_W - 1), state(POOL_BUF), state(SCONV_W - 1),
        ] + [cast_out(arr) for arr, _ in to_round],
        out_shape=[
            jax.ShapeDtypeStruct((M_P, D_MODEL), BF16),
            jax.ShapeDtypeStruct((BATCH, CONV_A_W - 1, D_GROUP), F32),
            jax.ShapeDtypeStruct((BATCH, POOL_BUF, D_GROUP), F32),
            jax.ShapeDtypeStruct((BATCH, SCONV_W - 1, D_GROUP), F32),
        ] + [jax.ShapeDtypeStruct(arr.shape[1:], BF16) for arr, _ in to_round],
        scratch_shapes=[
            pltpu.VMEM((N_SUB, HALO_A + TS, D_SUB), F32),
            pltpu.VMEM((N_SUB, HALO_P + TS, D_SUB), F32),
            pltpu.VMEM((N_SUB, HALO_C + TS, D_SUB), F32),
            pltpu.VMEM((TS, D_GROUP), F32),
        ] + [pltpu.VMEM((TS, D_GROUP), F32)] * N_PROJ + extra_scratch,
        compiler_params=_params(("arbitrary", "arbitrary")),
        name=f"mixer_prompt_l{layer}",
    )(xs, ss, w, p["conv_a_w"], p["conv_a_b"], p["ln_a_g"], p["ln_a_b"], p["pool_w"], p["pool_scale"],
      p["sconv_w"], p["sgu_ln_g"], p["sgu_ln_b"], p["sgu_w"], p["sgu_b_t"], p["out_norm_g"],
      *[arr for arr, _ in to_round])


N_SAMPLE_IN = 16
N_SAMPLE_OUT = 5


def _mixer_sample_kernel(*refs, layer):
    (wsm_ref, bsm_ref, proj_ref, sa_ref, sp_ref, sc_ref, caw_ref, cab_ref, lag_ref, lab_ref, pw_ref,
     ps_ref, scw_ref, sg_ref, sb_ref, ong_ref) = refs[:N_SAMPLE_IN]
    mix_ref, nsa_ref, nsp_ref, nsc_ref, vn_ref = refs[-N_SAMPLE_OUT:]

    def col(t, k):
        return proj_ref[t, :, k * D_GROUP:(k + 1) * D_GROUP]

    def slab(j):
        return slice(j * D_GROUP, (j + 1) * D_GROUP)

    steps = range(DEC_SEQ)

    glu = [col(t, 0) * _sigmoid(col(t, 1)) for t in steps]
    n_a = CONV_A_W - 1

    old_a = [sa_ref[j] for j in range(n_a)]

    def za(j):
        return old_a[j] if j < n_a else glu[j - n_a]

    cab = cab_ref[...]
    lag = lag_ref[...]
    lab = lab_ref[...]
    for t in steps:
        acc = caw_ref[0:1, :] * za(t)
        for k in range(1, CONV_A_W):
            acc = acc + caw_ref[k:k + 1, :] * za(t + k)
        ya = _layernorm(acc + cab, lag, lab)
        ya = ya * _sigmoid(ya)
        mix_ref[t, :, _mixer(0)] = _rms(ya, ong_ref[:, _mixer(0)]).astype(BF16)
    for j in range(n_a):
        nsa_ref[j] = za(j + DEC_SEQ)

    pin = [col(t, 2) for t in steps]

    old_p = [sp_ref[j] for j in range(POOL_BUF)]

    def zp(j):
        return old_p[j] if j < POOL_BUF else pin[j - POOL_BUF]

    ps = ps_ref[...]
    for t in steps:
        yb = []
        for g, w in enumerate(POOL_WINDOWS):
            cnt = float(min(PAST_LEN + t + 1, w))
            tot = zp(POOL_BUF + t)[:, _group(g)]
            for d in range(1, w):
                tot = tot + zp(POOL_BUF + t - d)[:, _group(g)]
            pooled = tot / cnt - pin[t][:, _group(g)]
            yb.append(jnp.dot(pooled.astype(BF16), pw_ref[g].astype(BF16), preferred_element_type=F32))
        yb = jnp.concatenate(yb, axis=-1) * ps
        mix_ref[t, :, _mixer(1)] = _rms(yb, ong_ref[:, _mixer(1)]).astype(BF16)
    for j in range(POOL_BUF):
        nsp_ref[j] = zp(j + DEC_SEQ)

    gated = [col(t, 5) * col(t, 3) for t in steps]
    n_c = SCONV_W - 1

    old_c = [sc_ref[j] for j in range(n_c)]

    def zc(j):
        return old_c[j] if j < n_c else gated[j - n_c]

    for t in steps:
        conv = scw_ref[0:1, :] * zc(t) + scw_ref[1:2, :] * zc(t + 1) + scw_ref[2:3, :] * zc(t + 2)
        yc = col(t, 4) * conv
        mix_ref[t, :, _mixer(2)] = _rms(yc, ong_ref[:, _mixer(2)]).astype(BF16)
    for j in range(n_c):
        nsc_ref[j] = zc(j + DEC_SEQ)

    sg = sg_ref[...]
    sb = sb_ref[...]
    vn = [_layernorm(col(t, 7), sg, sb) for t in steps]
    for t in steps:
        vn_ref[t] = vn[t]
    for t in steps:
        z = []
        for g in range(N_SUB):
            base = (layer * N_SUB + g) * DEC_SEQ * DEC_SEQ + t * DEC_SEQ
            zg = wsm_ref[base] * vn[0][:, _group(g)]
            for j in range(1, t + 1):
                zg = zg + wsm_ref[base + j] * vn[j][:, _group(g)]
            z.append(zg + bsm_ref[(layer * N_SUB + g) * DEC_SEQ + t])
        yd = col(t, 6) * jnp.concatenate(z, axis=-1)
        mix_ref[t, :, _mixer(3)] = _rms(yd, ong_ref[:, _mixer(3)]).astype(BF16)


def _mixer_sample(proj, states, stacked, p, layer):
    sa, sp, sc = states
    proj3 = proj.reshape(DEC_SEQ, DEC_BATCH, D_PROJ)
    in_place = tuple(stacked)

    def lp(shape):
        nd = len(shape)
        return pl.BlockSpec((None,) + shape, lambda i: (layer,) + (0,) * nd)

    smem = pl.BlockSpec(memory_space=pltpu.SMEM)
    st = lambda n: pl.BlockSpec((None, n, BB, D_GROUP), lambda i: (layer, 0, i, 0))
    st_shape = lambda n: jax.ShapeDtypeStruct((DEPTH, n, DEC_BATCH, D_GROUP), F32)
    n_a, n_c = CONV_A_W - 1, SCONV_W - 1
    outs = pl.pallas_call(
        functools.partial(_mixer_sample_kernel, layer=layer),
        grid=(DEC_BATCH // BB,),
        in_specs=[
            smem, smem,
            pl.BlockSpec((DEC_SEQ, BB, D_PROJ), lambda i: (0, i, 0)),
            st(n_a), st(POOL_BUF), st(n_c),
            lp((CONV_A_W, D_GROUP)), lp((1, D_GROUP)), lp((1, D_GROUP)), lp((1, D_GROUP)),
            lp((N_SUB, D_SUB, D_SUB)), lp((1, D_GROUP)), lp((SCONV_W, D_GROUP)),
            lp((1, D_GROUP)), lp((1, D_GROUP)), lp((1, D_MODEL)),
        ] + [pl.BlockSpec(memory_space=pl.ANY)] * len(in_place),
        out_specs=[
            pl.BlockSpec((DEC_SEQ, BB, D_MODEL), lambda i: (0, i, 0)),
            st(n_a), st(POOL_BUF), st(n_c), st(DEC_SEQ),
        ],
        out_shape=[
            jax.ShapeDtypeStruct((DEC_SEQ, DEC_BATCH, D_MODEL), BF16),
            st_shape(n_a), st_shape(POOL_BUF), st_shape(n_c), st_shape(DEC_SEQ),
        ],
        input_output_aliases={N_SAMPLE_IN + k: k + 1 for k in range(len(in_place))},
        compiler_params=_params(("arbitrary",)),
        name=f"mixer_sample_l{layer}",
    )(p["sgu_w_small"], p["sgu_b_small"], proj3, sa, sp, sc,
      p["conv_a_w"], p["conv_a_b"], p["ln_a_g"], p["ln_a_b"], p["pool_w"], p["pool_scale"],
      p["sconv_w"], p["sgu_ln_g"], p["sgu_ln_b"], p["out_norm_g"], *in_place)
    return outs[0].reshape(M_S, D_MODEL), tuple(outs[1:])


FINAL_ROWS = 1024


def _final_norm_kernel(x_ref, ss_ref, g_ref, o_ref):
    o_ref[...] = x_ref[...] * _row_scale(ss_ref) * g_ref[...]


def _final_norm(x, ss, g):
    def call(n_rows, first_row, tile, name):
        rows = lambda c: pl.BlockSpec((tile, c), lambda i: (first_row // tile + i, 0))
        return pl.pallas_call(
            _final_norm_kernel,
            grid=(n_rows // tile,),
            in_specs=[rows(D_MODEL), rows(1), pl.BlockSpec((1, D_MODEL), lambda i: (0, 0))],
            out_specs=pl.BlockSpec((tile, D_MODEL), lambda i: (i, 0)),
            out_shape=jax.ShapeDtypeStruct((n_rows, D_MODEL), F32),
            compiler_params=_params(("arbitrary",)),
            name=name,
        )(x, ss, g)

    y_p = call(M_P, 0, FINAL_ROWS, "final_norm_prompt")
    y_s = call(M_S, M_P, M_S, "final_norm_sample")
    return (y_p.reshape(BATCH, SEQ, D_MODEL),
            jnp.swapaxes(y_s.reshape(DEC_SEQ, DEC_BATCH, D_MODEL), 0, 1))


def kernel(x_prompt, x_sample, state_conv_a, state_pool, state_sconv, norm_mix_g, w_in, conv_a_w, conv_a_b, ln_a_g, ln_a_b, pool_w, pool_scale, sconv_w, sgu_ln_g, sgu_ln_b, sgu_w, sgu_b, out_norm_g, w_out, norm_ffn_g, w_gate, w_up, w_down, final_norm_g):
    row = lambda a: a.reshape(DEPTH, 1, a.shape[-1])
    p = {
        "conv_a_w": conv_a_w, "conv_a_b": row(conv_a_b), "ln_a_g": row(ln_a_g), "ln_a_b": row(ln_a_b),
        "pool_w": pool_w, "pool_scale": row(pool_scale), "sconv_w": sconv_w,
        "sgu_ln_g": row(sgu_ln_g), "sgu_ln_b": row(sgu_ln_b), "sgu_w": sgu_w,
        "sgu_b_t": jnp.swapaxes(sgu_b, 1, 2),
        "sgu_w_small": sgu_w[:, :, :DEC_SEQ, :DEC_SEQ].reshape(-1),
        "sgu_b_small": sgu_b[:, :, :DEC_SEQ].reshape(-1),
        "out_norm_g": row(out_norm_g),
    }
    g_mix = row(norm_mix_g)
    g_ffn = row(norm_ffn_g)
    w_in_b = w_in[0].astype(BF16)

    rows_major = lambda a: jnp.swapaxes(a, 1, 2)
    states = (rows_major(state_conv_a), rows_major(state_pool), rows_major(state_sconv))

    x = (x_prompt.reshape(M_P, D_MODEL), jnp.swapaxes(x_sample, 0, 1).reshape(M_S, D_MODEL))
    xs, ss = None, None

    ca_p, pl_p, sc_p = [], [], []
    stacked = ()
    for l in range(DEPTH):
        to_round = [(w_down, l), (w_out, l)] + ([(w_in, l + 1)] if l + 1 < DEPTH else [])
        lhs_p, lhs_s = ((x[0], g_mix), (x[1], g_mix)) if l == 0 else (xs, xs)
        mix_p, a_p, p_p, c_p, w_down_b, w_out_b, *w_in_next = _mixer_prompt(lhs_p, ss, w_in_b, to_round, p, l)
        proj_s = _in_proj_sample(lhs_s, ss, w_in_b, l)
        mix_s, stacked = _mixer_sample(proj_s, states, stacked, p, l)
        ca_p.append(a_p); pl_p.append(p_p); sc_p.append(c_p)
        x, xs, ss = _out_proj((mix_p, mix_s), w_out_b, x, g_ffn, l)
        h = _ffn_up(xs, ss, w_gate, w_up, l)
        x, xs, ss = _ffn_down(h, w_down_b, x, g_mix, l)
        w_in_b = w_in_next[0] if w_in_next else None

    y_prompt, y_sample = _final_norm(x, ss, final_norm_g.reshape(1, D_MODEL))
    ca_s, pl_s, sc_s, v_s = (rows_major(a) for a in stacked)
    return (y_prompt, y_sample, jnp.stack(ca_p), ca_s, jnp.stack(pl_p), pl_s, jnp.stack(sc_p), sc_s, v_s)
```

```python
import functools

import jax
import jax.numpy as jnp
from jax import lax
from jax.experimental import pallas as pl
from jax.experimental.pallas import tpu as pltpu

D_MODEL = 2048
BATCH = 4
SEQ = 2048
DEPTH = 4
DEC_BATCH = 128
DEC_SEQ = 4
PAST_LEN = 16384
D_GROUP = 512
N_SUB = 4
D_SUB = 128
N_PROJ = 8
D_PROJ = N_PROJ * D_GROUP
CONV_A_W = 31
POOL_WINDOWS = (2, 4, 8, 16)
POOL_BUF = 15
SCONV_W = 3
CHUNK = 128
D_FF = 5632
EPS = 1e-6

M_P = BATCH * SEQ
M_S = DEC_BATCH * DEC_SEQ
M_ALL = M_P + M_S

F32 = jnp.float32
BF16 = jnp.bfloat16

V7X_VMEM_LIMIT_BYTES = 56 * 1024 * 1024
V7X_VMEM_LIMIT_BIG_BYTES = 60 * 1024 * 1024

BM = 1088
BM_OUT = 512
BN_IN = 1024
BN_FF = 512
BN_DOWN = 512
CAST_ROWS = 256

BM_FF = 2176

TS = 512
RC = 128
HALO_A = 32
HALO_P = 16
HALO_C = 8

BB = 64


def _params(sem, vmem=V7X_VMEM_LIMIT_BYTES):
    return pltpu.CompilerParams(dimension_semantics=sem, vmem_limit_bytes=vmem)


def _cast_rows_to_bf16(w_ref, wb_ref):
    def body(i, carry):
        r = pl.multiple_of(i * CAST_ROWS, CAST_ROWS)
        wb_ref[pl.ds(r, CAST_ROWS), :] = w_ref[pl.ds(r, CAST_ROWS), :].astype(BF16)
        return carry

    lax.fori_loop(0, w_ref.shape[0] // CAST_ROWS, body, 0)


def _row_scale(ss_ref):
    return lax.rsqrt(ss_ref[...] * (1.0 / D_MODEL) + EPS)


def _sigmoid(x):
    return 0.5 * jnp.tanh(0.5 * x) + 0.5


def _two_source_specs(rows):
    n_p = M_P // rows
    return [pl.BlockSpec((rows, D_MODEL), lambda i: (jnp.minimum(i, n_p - 1), 0)),
            pl.BlockSpec((rows, D_MODEL), lambda i: (jnp.maximum(i - n_p, 0), 0))]


def _two_source_tile(xp_ref, xt_ref, rows):
    return jnp.where(pl.program_id(0) < M_P // rows, xp_ref[...], xt_ref[...])


def _in_proj_sample_kernel(xs_ref, ss_ref, w_ref, o_ref):
    o_ref[...] = jnp.dot(xs_ref[...], w_ref[...], preferred_element_type=F32) * _row_scale(ss_ref)


def _in_proj_sample_from_x_kernel(x_ref, g_ref, w_ref, o_ref):
    x = x_ref[...]
    scale = lax.rsqrt(jnp.mean(x * x, axis=-1, keepdims=True) + EPS)
    o_ref[...] = jnp.dot((x * g_ref[...]).astype(BF16), w_ref[...], preferred_element_type=F32) * scale


def _in_proj_sample(xs, ss, w, layer):
    if ss is None:
        body = _in_proj_sample_from_x_kernel
        operands = (xs[0], xs[1], w)
        lhs = [pl.BlockSpec((M_S, D_MODEL), lambda j: (0, 0)),
               pl.BlockSpec((None, 1, D_MODEL), lambda j: (layer, 0, 0))]
    else:
        first = M_P // M_S
        body = _in_proj_sample_kernel
        operands = (xs, ss, w)
        lhs = [pl.BlockSpec((M_S, D_MODEL), lambda j: (first, 0)),
               pl.BlockSpec((M_S, 1), lambda j: (first, 0))]
    return pl.pallas_call(
        body,
        grid=(D_PROJ // BN_IN,),
        in_specs=lhs + [pl.BlockSpec((D_MODEL, BN_IN), lambda j: (0, j))],
        out_specs=pl.BlockSpec((M_S, BN_IN), lambda j: (0, j)),
        out_shape=jax.ShapeDtypeStruct((M_S, D_PROJ), F32),
        compiler_params=_params(("arbitrary",)),
        name=f"in_proj_sample_l{layer}",
    )(*operands)


def _ffn_up_kernel(xs_ref, ss_ref, wg_ref, wu_ref, o_ref, wgb_ref, wub_ref):
    @pl.when(pl.program_id(1) == 0)
    def _():
        _cast_rows_to_bf16(wg_ref, wgb_ref)
        _cast_rows_to_bf16(wu_ref, wub_ref)

    scale = _row_scale(ss_ref)
    gate = jnp.dot(xs_ref[...], wgb_ref[...], preferred_element_type=F32) * scale
    up = jnp.dot(xs_ref[...], wub_ref[...], preferred_element_type=F32) * scale
    o_ref[...] = (gate * _sigmoid(gate) * up).astype(BF16)


def _ffn_up(xs, ss, wg, wu, layer):
    n_i = M_ALL // BM_FF
    row = lambda j, i: jnp.where(j % 2 == 0, i, n_i - 1 - i)
    return pl.pallas_call(
        _ffn_up_kernel,
        grid=(D_FF // BN_FF, n_i),
        in_specs=[
            pl.BlockSpec((BM_FF, D_MODEL), lambda j, i: (row(j, i), 0)),
            pl.BlockSpec((BM_FF, 1), lambda j, i: (row(j, i), 0)),
            pl.BlockSpec((None, D_MODEL, BN_FF), lambda j, i: (layer, 0, j)),
            pl.BlockSpec((None, D_MODEL, BN_FF), lambda j, i: (layer, 0, j)),
        ],
        out_specs=pl.BlockSpec((BM_FF, BN_FF), lambda j, i: (row(j, i), j)),
        out_shape=jax.ShapeDtypeStruct((M_ALL, D_FF), BF16),
        scratch_shapes=[pltpu.VMEM((D_MODEL, BN_FF), BF16), pltpu.VMEM((D_MODEL, BN_FF), BF16)],
        compiler_params=_params(("arbitrary", "arbitrary"), V7X_VMEM_LIMIT_BIG_BYTES),
        name=f"ffn_up_l{layer}",
    )(xs, ss, wg, wu)


def _out_proj_kernel(ap_ref, at_ref, w_ref, g_ref, *refs):
    *res_refs, x_ref, xs_ref, ss_ref = refs
    a = _two_source_tile(ap_ref, at_ref, BM_OUT)
    res = res_refs[0][...] if len(res_refs) == 1 else _two_source_tile(*res_refs, BM_OUT)
    o = res + jnp.dot(a, w_ref[...], preferred_element_type=F32)
    x_ref[...] = o
    xs_ref[...] = (o * g_ref[...]).astype(BF16)
    ss_ref[...] = jnp.sum(o * o, axis=-1, keepdims=True)


def _out_proj(mix, w, res, g_next, layer):
    rows = pl.BlockSpec((BM_OUT, D_MODEL), lambda i: (i, 0))
    res = res if isinstance(res, tuple) else (res,)
    return pl.pallas_call(
        _out_proj_kernel,
        grid=(M_ALL // BM_OUT,),
        in_specs=_two_source_specs(BM_OUT) + [
            pl.BlockSpec((D_MODEL, D_MODEL), lambda i: (0, 0), pipeline_mode=pl.Buffered(1)),
            pl.BlockSpec((None, 1, D_MODEL), lambda i: (layer, 0, 0)),
        ] + ([rows] if len(res) == 1 else _two_source_specs(BM_OUT)),
        out_specs=[rows, rows, pl.BlockSpec((BM_OUT, 1), lambda i: (i, 0))],
        out_shape=[
            jax.ShapeDtypeStruct((M_ALL, D_MODEL), F32),
            jax.ShapeDtypeStruct((M_ALL, D_MODEL), BF16),
            jax.ShapeDtypeStruct((M_ALL, 1), F32),
        ],
        compiler_params=_params(("arbitrary",)),
        name=f"out_proj_l{layer}",
    )(*mix, w, g_next, *res)


def _ffn_down_kernel(*refs, has_next):
    if has_next:
        h_ref, w_ref, r_ref, g_ref, x_ref, xs_ref, ss_ref = refs
    else:
        h_ref, w_ref, r_ref, x_ref, ss_ref = refs
    o = r_ref[...] + jnp.dot(h_ref[...], w_ref[...], preferred_element_type=F32)
    x_ref[...] = o
    if has_next:
        xs_ref[...] = (o * g_ref[...]).astype(BF16)
    part = jnp.sum(o * o, axis=-1, keepdims=True)
    j = pl.program_id(1)

    @pl.when(j == 0)
    def _():
        ss_ref[...] = part

    @pl.when(j > 0)
    def _():
        ss_ref[...] += part


def _ffn_down(h, w, res, g_next, layer):
    has_next = layer + 1 < DEPTH
    n_j = D_MODEL // BN_DOWN
    col = lambda i, j: jnp.where(i % 2 == 0, j, n_j - 1 - j)
    tile = pl.BlockSpec((BM, BN_DOWN), lambda i, j: (i, col(i, j)))
    ss_spec = pl.BlockSpec((BM, 1), lambda i, j: (i, 0))
    x_shape = jax.ShapeDtypeStruct((M_ALL, D_MODEL), F32)
    ss_shape = jax.ShapeDtypeStruct((M_ALL, 1), F32)
    in_specs = [pl.BlockSpec((BM, D_FF), lambda i, j: (i, 0)),
                pl.BlockSpec((D_FF, BN_DOWN), lambda i, j: (0, col(i, j))),
                tile]
    args = [h, w, res]
    out_specs, out_shape = [tile, ss_spec], [x_shape, ss_shape]
    if has_next:
        in_specs.append(pl.BlockSpec((None, 1, BN_DOWN), lambda i, j: (layer + 1, 0, col(i, j))))
        args.append(g_next)
        out_specs = [tile, tile, ss_spec]
        out_shape = [x_shape, jax.ShapeDtypeStruct((M_ALL, D_MODEL), BF16), ss_shape]
    outs = pl.pallas_call(
        functools.partial(_ffn_down_kernel, has_next=has_next),
        grid=(M_ALL // BM, D_MODEL // BN_DOWN),
        in_specs=in_specs,
        out_specs=out_specs,
        out_shape=out_shape,
        compiler_params=_params(("arbitrary", "arbitrary")),
        name=f"ffn_down_l{layer}",
    )(*args)
    return tuple(outs) if has_next else (outs[0], None, outs[1])


def _layernorm(x, g, b):
    mu = jnp.mean(x, axis=-1, keepdims=True)
    xc = x - mu
    var = jnp.mean(xc * xc, axis=-1, keepdims=True)
    return xc * lax.rsqrt(var + EPS) * g + b


def _rms(x, g):
    return x * lax.rsqrt(jnp.mean(x * x, axis=-1, keepdims=True) + EPS) * g


def _group(g):
    return slice(g * D_SUB, (g + 1) * D_SUB)


def _mixer(m):
    return slice(m * D_GROUP, (m + 1) * D_GROUP)


N_MIXER_IN = 15


def _mixer_prompt_kernel(*refs, n_cast, from_x):
    (xs_ref, ss_ref, w_ref, caw_ref, cab_ref, lag_ref, lab_ref, pw_ref, ps_ref, scw_ref,
     sg_ref, sb_ref, sw_ref, sbias_ref, ong_ref) = refs[:N_MIXER_IN]
    cast_in = refs[N_MIXER_IN:N_MIXER_IN + n_cast]
    mix_ref, sta_ref, stp_ref, stc_ref = refs[N_MIXER_IN + n_cast:N_MIXER_IN + n_cast + 4]
    cast_out = refs[N_MIXER_IN + n_cast + 4:N_MIXER_IN + 2 * n_cast + 4]
    gbuf, pbuf, cbuf, vbuf, *proj_refs = refs[N_MIXER_IN + 2 * n_cast + 4:]
    if from_x:
        x_ref, g_ref = xs_ref, ss_ref
        *proj_refs, xs_ref, ss_ref = proj_refs
    s = pl.program_id(1)
    streams = ((gbuf, HALO_A), (pbuf, HALO_P), (cbuf, HALO_C))

    @pl.when(s == 0)
    def _():
        for buf, halo in streams:
            buf[:, 0:halo, :] = jnp.zeros((N_SUB, halo, D_SUB), F32)

    @pl.when(s > 0)
    def _():
        for buf, halo in streams:
            buf[:, 0:halo, :] = buf[:, TS:TS + halo, :]

    if from_x:
        for c in range(TS // RC):
            x = x_ref[c * RC:(c + 1) * RC, :]
            xs_ref[c * RC:(c + 1) * RC, :] = (x * g_ref[...]).astype(BF16)
            ss_ref[c * RC:(c + 1) * RC, :] = jnp.sum(x * x, axis=-1, keepdims=True)
    scale = _row_scale(ss_ref)
    sg = sg_ref[...]
    sb = sb_ref[...]
    cab = cab_ref[...]
    lag = lag_ref[...]
    lab = lab_ref[...]
    ps = ps_ref[...]
    rows = lambda c: slice(c * RC, (c + 1) * RC)

    def project(k):
        proj_refs[k][...] = jnp.dot(xs_ref[...], w_ref[:, k * D_GROUP:(k + 1) * D_GROUP],
                                    preferred_element_type=F32) * scale

    def col(k, c, n=RC):
        return proj_refs[k][c * n:(c + 1) * n, :]

    def put(buf, halo, c, val):
        for j in range(N_SUB):
            buf[j, halo + c * RC:halo + (c + 1) * RC, :] = val[:, _group(j)]

    def window(buf, halo, j, c, back):
        lo = halo + c * RC - back
        return buf[j, lo:lo + RC, :]

    def stage_glu(c):
        put(gbuf, HALO_A, c, col(0, c) * _sigmoid(col(1, c)))

    def stage_pool(c):
        put(pbuf, HALO_P, c, col(2, c))

    def stage_gated(c):
        put(cbuf, HALO_C, c, col(5, c) * col(3, c))

    def stage_v(c):
        vbuf[rows(c), :] = _layernorm(col(7, c), sg, sb)

    def conv_module(c):
        tiles = []
        for j in range(N_SUB):
            acc = None
            for k in range(CONV_A_W):
                term = caw_ref[k:k + 1, _group(j)] * window(gbuf, HALO_A, j, c, CONV_A_W - 1 - k)
                acc = term if acc is None else acc + term
            tiles.append(acc)
        ya = _layernorm(jnp.concatenate(tiles, axis=-1) + cab, lag, lab)
        ya = ya * _sigmoid(ya)
        mix_ref[rows(c), _mixer(0)] = _rms(ya, ong_ref[:, _mixer(0)]).astype(BF16)

    def pooling(c):
        pos = s * TS + c * RC + lax.broadcasted_iota(jnp.int32, (RC, D_SUB), 0)
        yb = []
        for g, w in enumerate(POOL_WINDOWS):
            cur = window(pbuf, HALO_P, g, c, 0)
            tot = cur
            for d in range(1, w):
                tot = tot + window(pbuf, HALO_P, g, c, d)
            if c * RC + 1 >= w:
                pooled = tot / float(w) - cur
            else:
                pooled = tot / jnp.minimum(pos + 1, w).astype(F32) - cur
            yb.append(jnp.dot(pooled.astype(BF16), pw_ref[g].astype(BF16), preferred_element_type=F32))
        yb = jnp.concatenate(yb, axis=-1) * ps
        mix_ref[rows(c), _mixer(1)] = _rms(yb, ong_ref[:, _mixer(1)]).astype(BF16)

    def gated_conv(c):
        tiles = []
        for j in range(N_SUB):
            acc = None
            for k in range(SCONV_W):
                term = scw_ref[k:k + 1, _group(j)] * window(cbuf, HALO_C, j, c, SCONV_W - 1 - k)
                acc = term if acc is None else acc + term
            tiles.append(acc)
        yc = col(4, c) * jnp.concatenate(tiles, axis=-1)
        mix_ref[rows(c), _mixer(2)] = _rms(yc, ong_ref[:, _mixer(2)]).astype(BF16)

    row = lax.broadcasted_iota(jnp.int32, (CHUNK, CHUNK), 0)
    colid = lax.broadcasted_iota(jnp.int32, (CHUNK, CHUNK), 1)
    tril = (row >= colid).astype(F32)
    w_tril = [(sw_ref[g] * tril).astype(BF16) for g in range(N_SUB)]

    def spatial_gate(c):
        r0 = c * CHUNK
        v = vbuf[r0:r0 + CHUNK, :].astype(BF16)
        z = [jnp.dot(w_tril[g], v[:, _group(g)], preferred_element_type=F32) + sbias_ref[:, g:g + 1]
             for g in range(N_SUB)]
        yd = col(6, c, CHUNK) * jnp.concatenate(z, axis=-1)
        mix_ref[r0:r0 + CHUNK, _mixer(3)] = _rms(yd, ong_ref[:, _mixer(3)]).astype(BF16)

    every = lambda f: [functools.partial(f, c) for c in range(TS // RC)]
    convs = every(conv_module)
    gates = [functools.partial(spatial_gate, c) for c in range(TS // CHUNK)]
    plan = [
        (7, []),
        (0, every(stage_v)),
        (1, []),
        (6, every(stage_glu) + convs[0:1]),
        (5, convs[1:2] + gates[0:2]),
        (3, convs[2:3] + gates[2:4]),
        (4, convs[3:4] + every(stage_gated)),
        (2, convs[4:5] + every(gated_conv)),
        (None, convs[5:6] + every(stage_pool) + convs[6:8] + every(pooling)),
    ]
    for k, pieces in plan:
        if k is not None:
            project(k)
        for piece in pieces:
            piece()

    for src, dst in zip(cast_in, cast_out):
        dst[...] = src[...].astype(BF16)

    @pl.when(s == pl.num_programs(1) - 1)
    def _():
        def tail(buf, halo, n):
            return jnp.concatenate([buf[j, halo + TS - n:halo + TS, :] for j in range(N_SUB)], axis=-1)

        sta_ref[...] = tail(gbuf, HALO_A, CONV_A_W - 1)
        stp_ref[...] = tail(pbuf, HALO_P, POOL_BUF)
        stc_ref[...] = tail(cbuf, HALO_C, SCONV_W - 1)


def _mixer_prompt(xs, ss, w, to_round, p, layer):
    n_s = SEQ // TS
    steps = BATCH * n_s
    from_x = ss is None
    if from_x:
        xs, ss = xs
        lhs = [pl.BlockSpec((TS, D_MODEL), lambda b, s: (b * n_s + s, 0)),
               pl.BlockSpec((None, 1, D_MODEL), lambda b, s: (layer, 0, 0))]
        extra_scratch = [pltpu.VMEM((TS, D_MODEL), BF16), pltpu.VMEM((TS, 1), F32)]
    else:
        lhs = [pl.BlockSpec((TS, D_MODEL), lambda b, s: (b * n_s + s, 0)),
               pl.BlockSpec((TS, 1), lambda b, s: (b * n_s + s, 0))]
        extra_scratch = []

    def lp(shape):
        nd = len(shape)
        return pl.BlockSpec((None,) + shape, lambda b, s: (layer,) + (0,) * nd)

    def cast_in(arr, l):
        return pl.BlockSpec((None, arr.shape[1] // steps, arr.shape[2]), lambda b, s: (l, b * n_s + s, 0))

    def cast_out(arr):
        return pl.BlockSpec((arr.shape[1] // steps, arr.shape[2]), lambda b, s: (b * n_s + s, 0))

    state = lambda n: pl.BlockSpec((None, n, D_GROUP), lambda b, s: (b, 0, 0))
    return pl.pallas_call(
        functools.partial(_mixer_prompt_kernel, n_cast=len(to_round), from_x=from_x),
        grid=(BATCH, n_s),
        in_specs=lhs + [
            pl.BlockSpec((D_MODEL, D_PROJ), lambda b, s: (0, 0), pipeline_mode=pl.Buffered(1)),
            lp((CONV_A_W, D_GROUP)), lp((1, D_GROUP)), lp((1, D_GROUP)), lp((1, D_GROUP)),
            lp((N_SUB, D_SUB, D_SUB)), lp((1, D_GROUP)), lp((SCONV_W, D_GROUP)),
            lp((1, D_GROUP)), lp((1, D_GROUP)), lp((N_SUB, CHUNK, CHUNK)), lp((CHUNK, N_SUB)),
            lp((1, D_MODEL)),
        ] + [cast_in(arr, l) for arr, l in to_round],
        out_specs=[
            pl.BlockSpec((TS, D_MODEL), lambda b, s: (b * n_s + s, 0)),
            state(CONV_A_W - 1), state(POOL_BUF), state(SCONV_W - 1),
        ] + [cast_out(arr) for arr, _ in to_round],
        out_shape=[
            jax.ShapeDtypeStruct((M_P, D_MODEL), BF16),
            jax.ShapeDtypeStruct((BATCH, CONV_A_W - 1, D_GROUP), F32),
            jax.ShapeDtypeStruct((BATCH, POOL_BUF, D_GROUP), F32),
            jax.ShapeDtypeStruct((BATCH, SCONV_W - 1, D_GROUP), F32),
        ] + [jax.ShapeDtypeStruct(arr.shape[1:], BF16) for arr, _ in to_round],
        scratch_shapes=[
            pltpu.VMEM((N_SUB, HALO_A + TS, D_SUB), F32),
            pltpu.VMEM((N_SUB, HALO_P + TS, D_SUB), F32),
            pltpu.VMEM((N_SUB, HALO_C + TS, D_SUB), F32),
            pltpu.VMEM((TS, D_GROUP), F32),
        ] + [pltpu.VMEM((TS, D_GROUP), F32)] * N_PROJ + extra_scratch,
        compiler_params=_params(("arbitrary", "arbitrary")),
        name=f"mixer_prompt_l{layer}",
    )(xs, ss, w, p["conv_a_w"], p["conv_a_b"], p["ln_a_g"], p["ln_a_b"], p["pool_w"], p["pool_scale"],
      p["sconv_w"], p["sgu_ln_g"], p["sgu_ln_b"], p["sgu_w"], p["sgu_b_t"], p["out_norm_g"],
      *[arr for arr, _ in to_round])


N_SAMPLE_IN = 16
N_SAMPLE_OUT = 5


def _mixer_sample_kernel(*refs, layer):
    (wsm_ref, bsm_ref, proj_ref, sa_ref, sp_ref, sc_ref, caw_ref, cab_ref, lag_ref, lab_ref, pw_ref,
     ps_ref, scw_ref, sg_ref, sb_ref, ong_ref) = refs[:N_SAMPLE_IN]
    mix_ref, nsa_ref, nsp_ref, nsc_ref, vn_ref = refs[-N_SAMPLE_OUT:]

    def col(t, k):
        return proj_ref[t, :, k * D_GROUP:(k + 1) * D_GROUP]

    def slab(j):
        return slice(j * D_GROUP, (j + 1) * D_GROUP)

    steps = range(DEC_SEQ)

    glu = [col(t, 0) * _sigmoid(col(t, 1)) for t in steps]
    n_a = CONV_A_W - 1

    old_a = [sa_ref[j] for j in range(n_a)]

    def za(j):
        return old_a[j] if j < n_a else glu[j - n_a]

    cab = cab_ref[...]
    lag = lag_ref[...]
    lab = lab_ref[...]
    for t in steps:
        acc = caw_ref[0:1, :] * za(t)
        for k in range(1, CONV_A_W):
            acc = acc + caw_ref[k:k + 1, :] * za(t + k)
        ya = _layernorm(acc + cab, lag, lab)
        ya = ya * _sigmoid(ya)
        mix_ref[t, :, _mixer(0)] = _rms(ya, ong_ref[:, _mixer(0)]).astype(BF16)
    for j in range(n_a):
        nsa_ref[j] = za(j + DEC_SEQ)

    pin = [col(t, 2) for t in steps]

    old_p = [sp_ref[j] for j in range(POOL_BUF)]

    def zp(j):
        return old_p[j] if j < POOL_BUF else pin[j - POOL_BUF]

    ps = ps_ref[...]
    for t in steps:
        yb = []
        for g, w in enumerate(POOL_WINDOWS):
            cnt = float(min(PAST_LEN + t + 1, w))
            tot = zp(POOL_BUF + t)[:, _group(g)]
            for d in range(1, w):
                tot = tot + zp(POOL_BUF + t - d)[:, _group(g)]
            pooled = tot / cnt - pin[t][:, _group(g)]
            yb.append(jnp.dot(pooled.astype(BF16), pw_ref[g].astype(BF16), preferred_element_type=F32))
        yb = jnp.concatenate(yb, axis=-1) * ps
        mix_ref[t, :, _mixer(1)] = _rms(yb, ong_ref[:, _mixer(1)]).astype(BF16)
    for j in range(POOL_BUF):
        nsp_ref[j] = zp(j + DEC_SEQ)

    gated = [col(t, 5) * col(t, 3) for t in steps]
    n_c = SCONV_W - 1

    old_c = [sc_ref[j] for j in range(n_c)]

    def zc(j):
        return old_c[j] if j < n_c else gated[j - n_c]

    for t in steps:
        conv = scw_ref[0:1, :] * zc(t) + scw_ref[1:2, :] * zc(t + 1) + scw_ref[2:3, :] * zc(t + 2)
        yc = col(t, 4) * conv
        mix_ref[t, :, _mixer(2)] = _rms(yc, ong_ref[:, _mixer(2)]).astype(BF16)
    for j in range(n_c):
        nsc_ref[j] = zc(j + DEC_SEQ)

    sg = sg_ref[...]
    sb = sb_ref[...]
    vn = [_layernorm(col(t, 7), sg, sb) for t in steps]
    for t in steps:
        vn_ref[t] = vn[t]
    for t in steps:
        z = []
        for g in range(N_SUB):
            base = (layer * N_SUB + g) * DEC_SEQ * DEC_SEQ + t * DEC_SEQ
            zg = wsm_ref[base] * vn[0][:, _group(g)]
            for j in range(1, t + 1):
                zg = zg + wsm_ref[base + j] * vn[j][:, _group(g)]
            z.append(zg + bsm_ref[(layer * N_SUB + g) * DEC_SEQ + t])
        yd = col(t, 6) * jnp.concatenate(z, axis=-1)
        mix_ref[t, :, _mixer(3)] = _rms(yd, ong_ref[:, _mixer(3)]).astype(BF16)


def _mixer_sample(proj, states, stacked, p, layer):
    sa, sp, sc = states
    proj3 = proj.reshape(DEC_SEQ, DEC_BATCH, D_PROJ)
    in_place = tuple(stacked)

    def lp(shape):
        nd = len(shape)
        return pl.BlockSpec((None,) + shape, lambda i: (layer,) + (0,) * nd)

    smem = pl.BlockSpec(memory_space=pltpu.SMEM)
    st = lambda n: pl.BlockSpec((None, n, BB, D_GROUP), lambda i: (layer, 0, i, 0))
    st_shape = lambda n: jax.ShapeDtypeStruct((DEPTH, n, DEC_BATCH, D_GROUP), F32)
    n_a, n_c = CONV_A_W - 1, SCONV_W - 1
    outs = pl.pallas_call(
        functools.partial(_mixer_sample_kernel, layer=layer),
        grid=(DEC_BATCH // BB,),
        in_specs=[
            smem, smem,
            pl.BlockSpec((DEC_SEQ, BB, D_PROJ), lambda i: (0, i, 0)),
            st(n_a), st(POOL_BUF), st(n_c),
            lp((CONV_A_W, D_GROUP)), lp((1, D_GROUP)), lp((1, D_GROUP)), lp((1, D_GROUP)),
            lp((N_SUB, D_SUB, D_SUB)), lp((1, D_GROUP)), lp((SCONV_W, D_GROUP)),
            lp((1, D_GROUP)), lp((1, D_GROUP)), lp((1, D_MODEL)),
        ] + [pl.BlockSpec(memory_space=pl.ANY)] * len(in_place),
        out_specs=[
            pl.BlockSpec((DEC_SEQ, BB, D_MODEL), lambda i: (0, i, 0)),
            st(n_a), st(POOL_BUF), st(n_c), st(DEC_SEQ),
        ],
        out_shape=[
            jax.ShapeDtypeStruct((DEC_SEQ, DEC_BATCH, D_MODEL), BF16),
            st_shape(n_a), st_shape(POOL_BUF), st_shape(n_c), st_shape(DEC_SEQ),
        ],
        input_output_aliases={N_SAMPLE_IN + k: k + 1 for k in range(len(in_place))},
        compiler_params=_params(("arbitrary",)),
        name=f"mixer_sample_l{layer}",
    )(p["sgu_w_small"], p["sgu_b_small"], proj3, sa, sp, sc,
      p["conv_a_w"], p["conv_a_b"], p["ln_a_g"], p["ln_a_b"], p["pool_w"], p["pool_scale"],
      p["sconv_w"], p["sgu_ln_g"], p["sgu_ln_b"], p["out_norm_g"], *in_place)
    return outs[0].reshape(M_S, D_MODEL), tuple(outs[1:])


FINAL_ROWS = 1024


def _final_norm_kernel(x_ref, ss_ref, g_ref, o_ref):
    o_ref[...] = x_ref[...] * _row_scale(ss_ref) * g_ref[...]


def _final_norm(x, ss, g):
    def call(n_rows, first_row, tile, name):
        rows = lambda c: pl.BlockSpec((tile, c), lambda i: (first_row // tile + i, 0))
        return pl.pallas_call(
            _final_norm_kernel,
            grid=(n_rows // tile,),
            in_specs=[rows(D_MODEL), rows(1), pl.BlockSpec((1, D_MODEL), lambda i: (0, 0))],
            out_specs=pl.BlockSpec((tile, D_MODEL), lambda i: (i, 0)),
            out_shape=jax.ShapeDtypeStruct((n_rows, D_MODEL), F32),
            compiler_params=_params(("arbitrary",)),
            name=name,
        )(x, ss, g)

    y_p = call(M_P, 0, FINAL_ROWS, "final_norm_prompt")
    y_s = call(M_S, M_P, M_S, "final_norm_sample")
    return (y_p.reshape(BATCH, SEQ, D_MODEL),
            jnp.swapaxes(y_s.reshape(DEC_SEQ, DEC_BATCH, D_MODEL), 0, 1))


def kernel(x_prompt, x_sample, state_conv_a, state_pool, state_sconv, norm_mix_g, w_in, conv_a_w, conv_a_b, ln_a_g, ln_a_b, pool_w, pool_scale, sconv_w, sgu_ln_g, sgu_ln_b, sgu_w, sgu_b, out_norm_g, w_out, norm_ffn_g, w_gate, w_up, w_down, final_norm_g):
    row = lambda a: a.reshape(DEPTH, 1, a.shape[-1])
    p = {
        "conv_a_w": conv_a_w, "conv_a_b": row(conv_a_b), "ln_a_g": row(ln_a_g), "ln_a_b": row(ln_a_b),
        "pool_w": pool_w, "pool_scale": row(pool_scale), "sconv_w": sconv_w,
        "sgu_ln_g": row(sgu_ln_g), "sgu_ln_b": row(sgu_ln_b), "sgu_w": sgu_w,
        "sgu_b_t": jnp.swapaxes(sgu_b, 1, 2),
        "sgu_w_small": sgu_w[:, :, :DEC_SEQ, :DEC_SEQ].reshape(-1),
        "sgu_b_small": sgu_b[:, :, :DEC_SEQ].reshape(-1),
        "out_norm_g": row(out_norm_g),
    }
    g_mix = row(norm_mix_g)
    g_ffn = row(norm_ffn_g)
    w_in_b = w_in[0].astype(BF16)

    rows_major = lambda a: jnp.swapaxes(a, 1, 2)
    states = (rows_major(state_conv_a), rows_major(state_pool), rows_major(state_sconv))

    x = (x_prompt.reshape(M_P, D_MODEL), jnp.swapaxes(x_sample, 0, 1).reshape(M_S, D_MODEL))
    xs, ss = None, None

    ca_p, pl_p, sc_p = [], [], []
    stacked = ()
    for l in range(DEPTH):
        to_round = [(w_down, l), (w_out, l)] + ([(w_in, l + 1)] if l + 1 < DEPTH else [])
        lhs_p, lhs_s = ((x[0], g_mix), (x[1], g_mix)) if l == 0 else (xs, xs)
        mix_p, a_p, p_p, c_p, w_down_b, w_out_b, *w_in_next = _mixer_prompt(lhs_p, ss, w_in_b, to_round, p, l)
        proj_s = _in_proj_sample(lhs_s, ss, w_in_b, l)
        mix_s, stacked = _mixer_sample(proj_s, states, stacked, p, l)
        ca_p.append(a_p); pl_p.append(p_p); sc_p.append(c_p)
        x, xs, ss = _out_proj((mix_p, mix_s), w_out_b, x, g_ffn, l)
        h = _ffn_up(xs, ss, w_gate, w_up, l)
        x, xs, ss = _ffn_down(h, w_down_b, x, g_mix, l)
        w_in_b = w_in_next[0] if w_in_next else None

    y_prompt, y_sample = _final_norm(x, ss, final_norm_g.reshape(1, D_MODEL))
    ca_s, pl_s, sc_s, v_s = (rows_major(a) for a in stacked)
    return (y_prompt, y_sample, jnp.stack(ca_p), ca_s, jnp.stack(pl_p), pl_s, jnp.stack(sc_p), sc_s, v_s)
```
